```python
import jax, jax.numpy as jnp
from jax import lax
import numpy as np

D_MODEL = 1024
BATCH = 8
SEQ = 8192
DEPTH = 2

MEM_LEN = 256
N_EVEN = (DEPTH + 1) // 2
N_ODD = DEPTH // 2
A_HEADS = 4
A_KDIM = 128
A_VDIM = 128
A_CHUNK = 64
A_FDIM = A_HEADS * A_KDIM
A_WIDTH = A_HEADS * A_VDIM
B_WIDTH = 512
B_GROUPS = 4
B_CONV = 3
C_HEADS = 8
C_HEAD_DIM = 64
C_Q_RANK = 256
C_KV_RANK = 128
C_IDX_HEADS = 8
C_IDX_DIM = 64
C_MAX_TOPK = 256
C_QBLOCK = 128
C_WIDTH = C_HEADS * C_HEAD_DIM
D_GROUPS = 4
D_CHUNK = 128
D_WIDTH = 512
D_GROUP_DIM = D_WIDTH // D_GROUPS
X_HEADS = 4
X_HEAD_DIM = D_MODEL // X_HEADS
FF_DIM = ((8 * D_MODEL // 3 + 255) // 256) * 256

EVEN_SPLITS = (A_FDIM, A_FDIM, A_WIDTH, A_WIDTH, B_WIDTH, B_WIDTH, B_WIDTH)
EVEN_IN = sum(EVEN_SPLITS)
ODD_SPLITS = (C_Q_RANK, C_KV_RANK, C_IDX_DIM, C_IDX_HEADS, D_WIDTH, D_WIDTH)
ODD_IN = sum(ODD_SPLITS)

kernel_name = 'hybrid_hgrn2_shortconv_dsa_gmlp_trunk'

F32 = jnp.float32


def rms_norm(x, g, eps=1e-6):
    xf = x.astype(F32)
    y = xf * lax.rsqrt(jnp.mean(xf * xf, axis=-1, keepdims=True) + eps)
    return (y * g.astype(F32)).astype(x.dtype)


def layer_norm(x, g, b, eps=1e-5):
    xf = x.astype(F32)
    mu = jnp.mean(xf, axis=-1, keepdims=True)
    var = jnp.mean(jnp.square(xf - mu), axis=-1, keepdims=True)
    y = (xf - mu) * lax.rsqrt(var + eps)
    return (y * g.astype(F32) + b.astype(F32)).astype(x.dtype)


def split_cols(t, sizes):
    out, start = [], 0
    for s in sizes:
        out.append(t[..., start:start + s])
        start += s
    return out


def hgrn2_mixer(q_raw, f_raw, i_raw, g_raw, lb, out_gain):
    bn, s, _ = q_raw.shape
    nc = s // A_CHUNK
    q = jax.nn.silu(q_raw.astype(F32))
    f = lb + (1.0 - lb) * jax.nn.sigmoid(f_raw.astype(F32))
    k = 1.0 - f
    logf = jnp.log(f)
    v = i_raw.astype(F32)

    def chunks(t, d):
        return t.reshape(bn, nc, A_CHUNK, A_HEADS, d).transpose(1, 0, 3, 2, 4)

    xs = (chunks(q, A_KDIM), chunks(k, A_KDIM), chunks(logf, A_KDIM), chunks(v, A_VDIM))
    causal = jnp.tril(jnp.ones((A_CHUNK, A_CHUNK), bool))

    def step(state, inp):
        qc, kc, lfc, vc = inp
        b = jnp.cumsum(lfc, axis=2)
        o_inter = jnp.einsum('bhck,bhkv->bhcv', qc * jnp.exp(b), state)
        diff = b[:, :, :, None, :] - b[:, :, None, :, :]
        decay = jnp.exp(jnp.where(causal[:, :, None], diff, -jnp.inf))
        attn = jnp.einsum('bhtk,bhsk,bhtsk->bhts', qc, kc, decay)
        o_intra = jnp.einsum('bhts,bhsv->bhtv', attn, vc)
        b_last = b[:, :, -1:, :]
        new_state = jnp.exp(b_last[:, :, 0, :])[..., None] * state + jnp.einsum(
            'bhsk,bhsv->bhkv', kc * jnp.exp(b_last - b), vc)
        return new_state, o_inter + o_intra

    state0 = jnp.zeros((bn, A_HEADS, A_KDIM, A_VDIM), F32)
    _, o = lax.scan(step, state0, xs)
    o = o.transpose(1, 0, 3, 2, 4).reshape(bn, s, A_HEADS, A_VDIM)
    gate = jax.nn.silu(g_raw.astype(F32)).reshape(bn, s, A_HEADS, A_VDIM)
    o = rms_norm(o, out_gain) * gate
    return o.reshape(bn, s, A_WIDTH)


def short_conv_mixer(b_gate, c_gate, h, w_conv):
    z = c_gate * h
    y = lax.conv_general_dilated(
        z, w_conv[:, None, :].astype(z.dtype), window_strides=(1,),
        padding=[(B_CONV - 1, 0)], dimension_numbers=('NWC', 'WIO', 'NWC'),
        feature_group_count=B_WIDTH)
    return b_gate * y


def dsa_mixer(c_q_raw, c_kv_raw, k_idx_raw, w_idx_raw, q_norm_g, kv_norm_g,
              w_uq, w_uk, w_uv, idx_wq, idx_k_g, idx_k_b):
    bn, s, _ = c_q_raw.shape
    topk = min(C_MAX_TOPK, s // 4)
    nb = s // C_QBLOCK
    cq = rms_norm(c_q_raw, q_norm_g)
    ckv = rms_norm(c_kv_raw, kv_norm_g).astype(F32)
    q = (cq @ w_uq).reshape(bn, s, C_HEADS, C_HEAD_DIM)
    q_lat = jnp.einsum('bshd,hdc->bshc', q, w_uk).astype(F32) * (C_HEAD_DIM ** -0.5)
    q_idx = (cq @ idx_wq).reshape(bn, s, C_IDX_HEADS, C_IDX_DIM).astype(F32) * (C_IDX_DIM ** -0.5)
    k_idx = layer_norm(k_idx_raw, idx_k_g, idx_k_b).astype(F32)
    w_idx = w_idx_raw.astype(F32) * (C_IDX_HEADS ** -0.5)

    def blocks(t):
        return jnp.moveaxis(t.reshape((bn, nb, C_QBLOCK) + t.shape[2:]), 1, 0)

    starts = jnp.arange(nb, dtype=jnp.int32) * C_QBLOCK
    key_pos = jnp.arange(s, dtype=jnp.int32)

    def one_block(args):
        qb, qib, wb, start = args
        t_pos = start + jnp.arange(C_QBLOCK, dtype=jnp.int32)
        logits = jnp.einsum('bthd,bsd->bths', qib, k_idx)
        score = jnp.einsum('bths,bth->bts', jax.nn.relu(logits), wb)
        causal = key_pos[None, :] <= t_pos[:, None]
        score = jnp.where(causal[None], score, -jnp.inf)
        _, idx = lax.top_k(score, topk)
        valid = idx <= t_pos[None, :, None]
        kv_sel = jax.vmap(lambda c, i: c[i])(ckv, idx)
        sc = jnp.einsum('bthc,btkc->bthk', qb, kv_sel)
        sc = jnp.where(valid[:, :, None, :], sc, -jnp.inf)
        p = jax.nn.softmax(sc, axis=-1)
        return jnp.einsum('bthk,btkc->bthc', p, kv_sel)

    o_lat = lax.map(one_block, (blocks(q_lat), blocks(q_idx), blocks(w_idx), starts))
    o_lat = jnp.moveaxis(o_lat, 0, 1).reshape(bn, s, C_HEADS, C_KV_RANK)
    o = jnp.einsum('bshc,hcd->bshd', o_lat, w_uv.astype(F32))
    return o.reshape(bn, s, C_WIDTH)


def spatial_gating_mixer(u_raw, v_raw, v_g, v_b, w_s, b_s):
    bn, s, _ = u_raw.shape
    nch = s // D_CHUNK
    u = jax.nn.gelu(u_raw, approximate=False)
    v = layer_norm(jax.nn.gelu(v_raw, approximate=False), v_g, v_b)
    v = v.reshape(bn, nch, D_CHUNK, D_GROUPS, D_GROUP_DIM)
    w_causal = w_s * jnp.tril(jnp.ones((D_CHUNK, D_CHUNK), w_s.dtype))
    mixed = jnp.einsum('gts,bnsgc->bntgc', w_causal, v) + b_s.T[:, :, None]
    return u * mixed.reshape(bn, s, D_WIDTH)


def memory_cross_attention(h, mem, mem_g, wq, wkv, wo):
    bn, s, _ = h.shape
    m_len = mem.shape[1]
    m = rms_norm(mem, mem_g)
    q = (h @ wq).reshape(bn, s, X_HEADS, X_HEAD_DIM)
    kv = (m @ wkv).reshape(bn, m_len, 2, X_HEADS, X_HEAD_DIM)
    k, v = kv[:, :, 0], kv[:, :, 1]
    sc = jnp.einsum('bshd,bmhd->bhsm', q, k).astype(F32) * (X_HEAD_DIM ** -0.5)
    p = jax.nn.softmax(sc, axis=-1).astype(v.dtype)
    o = jnp.einsum('bhsm,bmhd->bshd', p, v).reshape(bn, s, D_MODEL)
    return o @ wo


def swiglu_ffn(h, w13, w2):
    gu = h @ w13
    g, u = gu[..., :FF_DIM], gu[..., FF_DIM:]
    return (jax.nn.silu(g) * u) @ w2


def setup_inputs(seed: int = 0) -> dict:
    key = jax.random.key(seed)
    ks = iter(jax.random.split(key, 64))

    def nrm(shape, fan_in):
        return jax.random.normal(next(ks), shape, F32) * (fan_in ** -0.5)

    def gain(shape):
        return 1.0 + 0.02 * jax.random.normal(next(ks), shape, F32)

    def small(shape, scale=0.02):
        return scale * jax.random.normal(next(ks), shape, F32)

    return {
        'x': jax.random.normal(next(ks), (BATCH, SEQ, D_MODEL), F32),
        'mem': jax.random.normal(next(ks), (BATCH, MEM_LEN, D_MODEL), F32),
        'hgrn_lower_bounds': small((DEPTH + 1, A_FDIM), 0.1),
        'even_mix_norm': gain((N_EVEN, D_MODEL)),
        'even_w_in': nrm((N_EVEN, D_MODEL, EVEN_IN), D_MODEL),
        'even_a_out_norm': gain((N_EVEN, A_HEADS, A_VDIM)),
        'even_b_conv': nrm((N_EVEN, B_CONV, B_WIDTH), B_CONV),
        'even_w_out': nrm((N_EVEN, A_WIDTH + B_WIDTH, D_MODEL), A_WIDTH + B_WIDTH),
        'odd_mix_norm': gain((N_ODD, D_MODEL)),
        'odd_w_in': nrm((N_ODD, D_MODEL, ODD_IN), D_MODEL),
        'odd_c_q_norm': gain((N_ODD, C_Q_RANK)),
        'odd_c_kv_norm': gain((N_ODD, C_KV_RANK)),
        'odd_c_w_uq': nrm((N_ODD, C_Q_RANK, C_HEADS * C_HEAD_DIM), C_Q_RANK),
        'odd_c_w_uk': nrm((N_ODD, C_HEADS, C_HEAD_DIM, C_KV_RANK), C_KV_RANK),
        'odd_c_w_uv': nrm((N_ODD, C_HEADS, C_KV_RANK, C_HEAD_DIM), C_KV_RANK),
        'odd_c_idx_wq': nrm((N_ODD, C_Q_RANK, C_IDX_HEADS * C_IDX_DIM), C_Q_RANK),
        'odd_c_idx_k_g': gain((N_ODD, C_IDX_DIM)),
        'odd_c_idx_k_b': small((N_ODD, C_IDX_DIM)),
        'odd_d_v_g': gain((N_ODD, D_WIDTH)),
        'odd_d_v_b': small((N_ODD, D_WIDTH)),
        'odd_d_w_s': nrm((N_ODD, D_GROUPS, D_CHUNK, D_CHUNK), D_CHUNK),
        'odd_d_b_s': gain((N_ODD, D_GROUPS, D_CHUNK)),
        'odd_w_out': nrm((N_ODD, C_WIDTH + D_WIDTH, D_MODEL), C_WIDTH + D_WIDTH),
        'xa_norm': gain((DEPTH, D_MODEL)),
        'xa_mem_norm': gain((DEPTH, D_MODEL)),
        'xa_wq': nrm((DEPTH, D_MODEL, D_MODEL), D_MODEL),
        'xa_wkv': nrm((DEPTH, D_MODEL, 2 * D_MODEL), D_MODEL),
        'xa_wo': nrm((DEPTH, D_MODEL, D_MODEL), D_MODEL),
        'ffn_norm': gain((DEPTH, D_MODEL)),
        'ffn_w13': nrm((DEPTH, D_MODEL, 2 * FF_DIM), D_MODEL),
        'ffn_w2': nrm((DEPTH, FF_DIM, D_MODEL), FF_DIM),
        'final_norm': gain((D_MODEL,)),
    }


def reference(x, mem, hgrn_lower_bounds, even_mix_norm, even_w_in, even_a_out_norm,
              even_b_conv, even_w_out, odd_mix_norm, odd_w_in, odd_c_q_norm, odd_c_kv_norm,
              odd_c_w_uq, odd_c_w_uk, odd_c_w_uv, odd_c_idx_wq, odd_c_idx_k_g, odd_c_idx_k_b,
              odd_d_v_g, odd_d_v_b, odd_d_w_s, odd_d_b_s, odd_w_out, xa_norm, xa_mem_norm,
              xa_wq, xa_wkv, xa_wo, ffn_norm, ffn_w13, ffn_w2, final_norm):
    lower_bounds = jnp.cumsum(jax.nn.softmax(hgrn_lower_bounds.astype(F32), axis=0), axis=0)
    for layer in range(DEPTH):
        j = layer // 2
        if layer % 2 == 0:
            h = rms_norm(x, even_mix_norm[j])
            proj = h @ even_w_in[j]
            aq, af, ai, ag, bb, bc, bh = split_cols(proj, EVEN_SPLITS)
            o_a = hgrn2_mixer(aq, af, ai, ag, lower_bounds[layer], even_a_out_norm[j])
            o_b = short_conv_mixer(bb, bc, bh, even_b_conv[j])
            mixed = jnp.concatenate([o_a.astype(x.dtype), o_b.astype(x.dtype)], axis=-1) @ even_w_out[j]
        else:
            h = rms_norm(x, odd_mix_norm[j])
            proj = h @ odd_w_in[j]
            cq, ckv, kidx, widx, du, dv = split_cols(proj, ODD_SPLITS)
            o_c = dsa_mixer(cq, ckv, kidx, widx, odd_c_q_norm[j], odd_c_kv_norm[j],
                            odd_c_w_uq[j], odd_c_w_uk[j], odd_c_w_uv[j], odd_c_idx_wq[j],
                            odd_c_idx_k_g[j], odd_c_idx_k_b[j])
            o_d = spatial_gating_mixer(du, dv, odd_d_v_g[j], odd_d_v_b[j], odd_d_w_s[j], odd_d_b_s[j])
            mixed = jnp.concatenate([o_c.astype(x.dtype), o_d.astype(x.dtype)], axis=-1) @ odd_w_out[j]
        x = x + mixed.astype(x.dtype)
        x = x + memory_cross_attention(rms_norm(x, xa_norm[layer]), mem, xa_mem_norm[layer],
                                       xa_wq[layer], xa_wkv[layer], xa_wo[layer]).astype(x.dtype)
        x = x + swiglu_ffn(rms_norm(x, ffn_norm[layer]), ffn_w13[layer], ffn_w2[layer]).astype(x.dtype)
    return rms_norm(x, final_norm)
```

```python
import functools

import jax
import jax.numpy as jnp
import numpy as np
from jax import lax
from jax.experimental import pallas as pl
from jax.experimental.pallas import tpu as pltpu

F32 = jnp.float32
BF16 = jnp.bfloat16
I32 = jnp.int32

V7X_VMEM_BYTES = 64 * 1024 * 1024
VMEM_LIMIT = 56 * 1024 * 1024
LANES = 128

A_HEADS, A_KDIM, A_VDIM, A_CHUNK = 4, 128, 128, 64
A_WIDTH = A_HEADS * A_VDIM
B_WIDTH = 512
C_HEADS, C_HEAD_DIM, C_Q_RANK, C_KV_RANK = 8, 64, 256, 128
C_IDX_HEADS, C_IDX_DIM, C_MAX_TOPK, C_QBLOCK = 8, 64, 256, 128
D_GROUPS, D_CHUNK, D_WIDTH = 4, 128, 512
X_HEADS = 4
HGRN_LEVELS = 6
DSA_KEY_CHUNK = 512
INT_MIN = -(2 ** 31)


def _dot(a, b):
    return jnp.dot(a, b, preferred_element_type=F32)


def _dot_nt(a, b):
    return lax.dot_general(a, b, (((1,), (1,)), ((), ())), preferred_element_type=F32)


def _dot_tn(a, b):
    return lax.dot_general(a, b, (((0,), (0,)), ((), ())), preferred_element_type=F32)


def _rms(x, g, eps=1e-6):
    return x * lax.rsqrt(jnp.mean(x * x, axis=-1, keepdims=True) + eps) * g


def _layer_norm(x, g, b, eps=1e-5):
    mu = jnp.mean(x, axis=-1, keepdims=True)
    xc = x - mu
    var = jnp.mean(xc * xc, axis=-1, keepdims=True)
    return xc * lax.rsqrt(var + eps) * g + b


def _silu(x):
    return x * jax.nn.sigmoid(x)


def _gelu(x):
    return 0.5 * x * (1.0 + lax.erf(x * (2.0 ** -0.5)))


def _const_spec(shape):
    zeros = (0,) * len(shape)
    return pl.BlockSpec(shape, lambda *_: zeros, pipeline_mode=pl.Buffered(1))


def _params(n_parallel=0, n_arbitrary=2):
    return pltpu.CompilerParams(
        dimension_semantics=("parallel",) * n_parallel + ("arbitrary",) * n_arbitrary,
        vmem_limit_bytes=VMEM_LIMIT)


def _mem_kv_kernel(mem_ref, g_ref, wkv_ref, k_ref, v_ref):
    d = mem_ref.shape[-1]
    m = _rms(mem_ref[0], g_ref[0]).astype(BF16)
    kv = _dot(m, wkv_ref[0])
    k_ref[0, 0] = kv[:, :d].astype(BF16)
    v_ref[0, 0] = kv[:, d:].astype(BF16)


def _mem_kv(mem, mem_g, wkv):
    depth, d = mem_g.shape
    bn, m_len, _ = mem.shape
    out = jax.ShapeDtypeStruct((depth, bn, m_len, d), BF16)
    return pl.pallas_call(
        _mem_kv_kernel,
        grid=(depth, bn),
        in_specs=[pl.BlockSpec((1, m_len, d), lambda l, b: (b, 0, 0)),
                  pl.BlockSpec((1, 1, d), lambda l, b: (l, 0, 0)),
                  pl.BlockSpec((1, d, 2 * d), lambda l, b: (l, 0, 0))],
        out_specs=[pl.BlockSpec((1, 1, m_len, d), lambda l, b: (l, b, 0, 0))] * 2,
        out_shape=[out, out],
        compiler_params=_params(),
        name="mem_kv",
    )(mem, mem_g.reshape(depth, 1, d), wkv.astype(BF16))


def _xattn_ffn_kernel(x_ref, k_ref, v_ref, gxa_ref, wq_ref, wo_ref, gff_ref, w13_ref, w2_ref, gfin_ref,
                      o_ref, *, final_norm, ff_chunk):
    x = x_ref[0]
    d = x.shape[-1]
    hd = d // X_HEADS
    ff = w2_ref.shape[0]
    h = _rms(x, gxa_ref[...]).astype(BF16)
    q = _dot(h, wq_ref[...])
    heads = []
    for i in range(X_HEADS):
        cols = slice(i * hd, (i + 1) * hd)
        sc = _dot_nt(q[:, cols].astype(BF16), k_ref[0, :, cols]) * (hd ** -0.5)
        p = jnp.exp(sc - jnp.max(sc, axis=-1, keepdims=True))
        o = _dot(p.astype(BF16), v_ref[0, :, cols])
        heads.append((o / jnp.sum(p, axis=-1, keepdims=True)).astype(BF16))
    x = x + _dot(jnp.concatenate(heads, axis=-1), wo_ref[...])
    h2 = _rms(x, gff_ref[...]).astype(BF16)
    acc = jnp.zeros_like(x)
    for c0 in range(0, ff, ff_chunk):
        g = _dot(h2, w13_ref[:, c0:c0 + ff_chunk])
        u = _dot(h2, w13_ref[:, ff + c0:ff + c0 + ff_chunk])
        acc = acc + _dot((_silu(g) * u).astype(BF16), w2_ref[c0:c0 + ff_chunk, :])
    x = x + acc
    if final_norm:
        x = _rms(x, gfin_ref[...])
    o_ref[0] = x


def _xattn_ffn(x, k_mem, v_mem, g_xa, wq, wo, g_ff, w13, w2, g_fin, *, final_norm, tm):
    bn, s, d = x.shape
    m_len = k_mem.shape[1]
    ff = w2.shape[0]
    ff_chunk = ff // 2 if (ff // 2) % LANES == 0 else ff
    row = lambda a: a.reshape(1, d)
    kern = functools.partial(_xattn_ffn_kernel, final_norm=final_norm, ff_chunk=ff_chunk)
    return pl.pallas_call(
        kern,
        grid=(bn, s // tm),
        in_specs=[pl.BlockSpec((1, tm, d), lambda b, i: (b, i, 0)),
                  pl.BlockSpec((1, m_len, d), lambda b, i: (b, 0, 0)),
                  pl.BlockSpec((1, m_len, d), lambda b, i: (b, 0, 0)),
                  _const_spec((1, d)), _const_spec((d, d)), _const_spec((d, d)),
                  _const_spec((1, d)), _const_spec((d, 2 * ff)), _const_spec((ff, d)),
                  _const_spec((1, d))],
        out_specs=pl.BlockSpec((1, tm, d), lambda b, i: (b, i, 0)),
        out_shape=jax.ShapeDtypeStruct(x.shape, F32),
        compiler_params=_params(),
        name="xattn_ffn_final" if final_norm else "xattn_ffn",
    )(x, k_mem, v_mem, row(g_xa), wq.astype(BF16), wo.astype(BF16), row(g_ff),
      w13.astype(BF16), w2.astype(BF16), row(g_fin))


def _hgrn_tables():
    n = A_CHUNK
    t = np.arange(n)[:, None]
    u = np.arange(n)[None, :]
    sums, pair = [], []
    for lvl in range(HGRN_LEVELS):
        c = n >> (lvl + 1)
        mid = (t // (2 * c)) * (2 * c) + c
        late = t >= mid
        sums.append(np.where(late, (u >= mid) & (u <= t), (u > t) & (u < mid)))
        pair.append(late & (u < mid) & (u >= mid - c))
    sums.append(u <= t)
    sums.append(u > t)
    pair.append(t == u)
    return (np.concatenate(sums, axis=0).astype(np.float32),
            np.stack(pair, axis=0).astype(np.float32))


def _even_kernel(x_ref, lbp_ref, g_ref, win_ref, aon_ref, conv_ref, wout_ref, sums_ref, pair_ref,
                 o_ref, q_scr, k_scr, v_scr, lfh_scr, lfl_scr, oa_scr, z_scr, st_scr):
    tm = x_ref.shape[1]
    n = A_CHUNK

    @pl.when(pl.program_id(1) == 0)
    def _():
        st_scr[...] = jnp.zeros_like(st_scr)
        z_scr[0:8, :] = jnp.zeros((8, B_WIDTH), F32)

    x = x_ref[0]
    h = _rms(x, g_ref[...]).astype(BF16)

    def proj(j, width=A_WIDTH):
        return _dot(h, win_ref[:, j * width:(j + 1) * width])

    lbp = lbp_ref[...]
    e = jnp.exp(lbp - jnp.max(lbp, axis=0, keepdims=True))
    lb = e[0:1, :] / jnp.sum(e, axis=0, keepdims=True)

    q_scr[...] = _silu(proj(0))
    f = lb + (1.0 - lb) * jax.nn.sigmoid(proj(1))
    k_scr[...] = 1.0 - f
    logf = jnp.log(f)
    lf_hi = logf.astype(BF16)
    lfh_scr[...] = lf_hi
    lfl_scr[...] = (logf - lf_hi.astype(F32)).astype(BF16)
    v_scr[...] = proj(2)

    row = lax.broadcasted_iota(I32, (n, 1), 0)

    def chunk(c, carry):
        r0 = pl.multiple_of(c * n, n)
        rows = pl.ds(r0, n)
        sums = sums_ref[...]
        ex = jnp.exp(_dot(sums, lfh_scr[rows, :]) + _dot(sums, lfl_scr[rows, :]))
        qc, kc, vc = q_scr[rows, :], k_scr[rows, :], v_scr[rows, :]
        for hh in range(A_HEADS):
            cols = slice(hh * A_KDIM, (hh + 1) * A_KDIM)
            qh, kh = qc[:, cols], kc[:, cols]
            attn = _dot_nt(qh.astype(BF16), kh.astype(BF16)) * pair_ref[HGRN_LEVELS]
            for lvl in range(HGRN_LEVELS):
                late = (row & (n >> (lvl + 1))) != 0
                xl = (jnp.where(late, qh, kh) * ex[lvl * n:(lvl + 1) * n, cols]).astype(BF16)
                attn = attn + _dot_nt(xl, xl) * pair_ref[lvl]
            e_in = ex[HGRN_LEVELS * n:(HGRN_LEVELS + 1) * n, cols]
            e_out = ex[(HGRN_LEVELS + 1) * n:(HGRN_LEVELS + 2) * n, cols]
            st = st_scr[hh]
            vh = vc[:, cols].astype(BF16)
            o = _dot(attn.astype(BF16), vh) + _dot_nt((qh * e_in).astype(BF16), st.astype(BF16))
            oa_scr[rows, cols] = o
            st_scr[hh] = st * e_in[n - 1:n, :] + _dot_tn(vh, (kh * e_out).astype(BF16))
        return carry

    lax.fori_loop(0, tm // n, chunk, 0)

    gate = _silu(proj(3))
    parts = []
    for hh in range(A_HEADS):
        cols = slice(hh * A_VDIM, (hh + 1) * A_VDIM)
        parts.append((_rms(oa_scr[:, cols], aon_ref[hh:hh + 1, :]) * gate[:, cols]).astype(BF16))

    z_scr[8:8 + tm, :] = proj(5) * proj(6)
    w = conv_ref[...]
    y = (w[0:1, :] * z_scr[6:6 + tm, :] + w[1:2, :] * z_scr[7:7 + tm, :] + w[2:3, :] * z_scr[8:8 + tm, :])
    z_scr[0:8, :] = z_scr[tm:tm + 8, :]
    parts.append((proj(4) * y).astype(BF16))

    o_ref[0] = x + _dot(jnp.concatenate(parts, axis=-1), wout_ref[...])


def _even_mixer(x, lbp, g, w_in, a_out_norm, b_conv, w_out, *, tm):
    bn, s, d = x.shape
    sums, pair = _hgrn_tables()
    n_in = w_in.shape[1]
    return pl.pallas_call(
        _even_kernel,
        grid=(bn, s // tm),
        in_specs=[pl.BlockSpec((1, tm, d), lambda b, i: (b, i, 0)),
                  _const_spec(lbp.shape), _const_spec((1, d)), _const_spec((d, n_in)),
                  _const_spec(a_out_norm.shape), _const_spec(b_conv.shape),
                  _const_spec(w_out.shape), _const_spec(sums.shape), _const_spec(pair.shape)],
        out_specs=pl.BlockSpec((1, tm, d), lambda b, i: (b, i, 0)),
        out_shape=jax.ShapeDtypeStruct(x.shape, F32),
        scratch_shapes=[pltpu.VMEM((tm, A_WIDTH), F32), pltpu.VMEM((tm, A_WIDTH), F32),
                        pltpu.VMEM((tm, A_WIDTH), F32), pltpu.VMEM((tm, A_WIDTH), BF16),
                        pltpu.VMEM((tm, A_WIDTH), BF16), pltpu.VMEM((tm, A_WIDTH), F32),
                        pltpu.VMEM((tm + 8, B_WIDTH), F32),
                        pltpu.VMEM((A_HEADS, A_VDIM, A_KDIM), F32)],
        compiler_params=_params(),
        name="even_mixer",
    )(x, lbp, g.reshape(1, d), w_in.astype(BF16), a_out_norm, b_conv, w_out.astype(BF16),
      jnp.asarray(sums, BF16), jnp.asarray(pair, F32))


def _odd_proj_kernel(x_ref, g_ref, win_ref, qg_ref, kvg_ref, wuq_ref, wuk_ref, iwq_ref, ikg_ref, ikb_ref,
                     vg_ref, vb_ref, ws_ref, bs_ref,
                     qlat_ref, qidx_ref, widx_ref, ckv_ref, kidx_ref, od_ref):
    tm = x_ref.shape[1]
    h = _rms(x_ref[0], g_ref[...]).astype(BF16)
    o = 0

    def proj(width):
        nonlocal o
        r = _dot(h, win_ref[:, o:o + width])
        o += width
        return r

    cq = _rms(proj(C_Q_RANK), qg_ref[...]).astype(BF16)
    ckv_ref[0] = _rms(proj(C_KV_RANK), kvg_ref[...]).astype(BF16)
    du = proj(D_WIDTH)
    dv = proj(D_WIDTH)
    small = proj(LANES)
    kidx_ref[0] = _layer_norm(small[:, :C_IDX_DIM], ikg_ref[...], ikb_ref[...]).astype(BF16)
    widx_ref[0] = small[:, C_IDX_DIM:C_IDX_DIM + C_IDX_HEADS] * (C_IDX_HEADS ** -0.5)

    q = _dot(cq, wuq_ref[...])
    lat = [_dot(q[:, i * C_HEAD_DIM:(i + 1) * C_HEAD_DIM].astype(BF16), wuk_ref[i]) * (C_HEAD_DIM ** -0.5)
           for i in range(C_HEADS)]
    qlat_ref[0] = jnp.concatenate(lat, axis=-1).astype(BF16)
    qidx_ref[0] = (_dot(cq, iwq_ref[...]) * (C_IDX_DIM ** -0.5)).astype(BF16)

    u = _gelu(du)
    v = _layer_norm(_gelu(dv), vg_ref[...], vb_ref[...]).astype(BF16)
    tri = (lax.broadcasted_iota(I32, (D_CHUNK, D_CHUNK), 0) >= lax.broadcasted_iota(I32, (D_CHUNK, D_CHUNK), 1))
    gw = D_WIDTH // D_GROUPS
    for gi in range(D_GROUPS):
        wc = jnp.where(tri, ws_ref[gi], 0.0).astype(BF16)
        cols = slice(gi * gw, (gi + 1) * gw)
        for c0 in range(0, tm, D_CHUNK):
            mixed = _dot(wc, v[c0:c0 + D_CHUNK, cols]) + bs_ref[:, gi:gi + 1]
            od_ref[0, c0:c0 + D_CHUNK, cols] = (u[c0:c0 + D_CHUNK, cols] * mixed).astype(BF16)


def _odd_proj(x, g, w_in, q_g, kv_g, w_uq, w_uk, idx_wq, ik_g, ik_b, v_g, v_b, w_s, b_s, *, tm):
    bn, s, d = x.shape
    c0, c1, c2, c3 = C_Q_RANK, C_Q_RANK + C_KV_RANK, C_Q_RANK + C_KV_RANK + C_IDX_DIM, \
        C_Q_RANK + C_KV_RANK + C_IDX_DIM + C_IDX_HEADS
    pad = LANES - C_IDX_DIM - C_IDX_HEADS
    w = jnp.concatenate([w_in[:, :c1], w_in[:, c3:], w_in[:, c1:c3], jnp.zeros((d, pad), w_in.dtype)],
                        axis=1).astype(BF16)
    row = lambda a: a.reshape(1, -1)
    shp = lambda n, dt: jax.ShapeDtypeStruct((bn, s, n), dt)
    blk = lambda n: pl.BlockSpec((1, tm, n), lambda b, i: (b, i, 0))
    lat_w = C_HEADS * C_KV_RANK
    idx_w = C_IDX_HEADS * C_IDX_DIM
    return pl.pallas_call(
        _odd_proj_kernel,
        grid=(bn, s // tm),
        in_specs=[blk(d), _const_spec((1, d)), _const_spec(w.shape),
                  _const_spec((1, C_Q_RANK)), _const_spec((1, C_KV_RANK)),
                  _const_spec(w_uq.shape), _const_spec(w_uk.shape), _const_spec(idx_wq.shape),
                  _const_spec((1, C_IDX_DIM)), _const_spec((1, C_IDX_DIM)),
                  _const_spec((1, D_WIDTH)), _const_spec((1, D_WIDTH)),
                  _const_spec(w_s.shape), _const_spec((D_CHUNK, D_GROUPS))],
        out_specs=[blk(lat_w), blk(idx_w), blk(C_IDX_HEADS), blk(C_KV_RANK), blk(C_IDX_DIM), blk(D_WIDTH)],
        out_shape=[shp(lat_w, BF16), shp(idx_w, BF16), shp(C_IDX_HEADS, F32), shp(C_KV_RANK, BF16),
                   shp(C_IDX_DIM, BF16), shp(D_WIDTH, BF16)],
        compiler_params=_params(),
        name="odd_proj",
    )(x, row(g), w, row(q_g), row(kv_g), w_uq.astype(BF16), w_uk.astype(BF16), idx_wq.astype(BF16),
      row(ik_g), row(ik_b), row(v_g), row(v_b), w_s, b_s.T)


def _order_key(score):
    bits = lax.bitcast_convert_type(score + 0.0, I32)
    return bits ^ ((bits >> 31) & 0x7FFFFFFF)


def _dsa_kernel(x_ref, qlat_ref, qidx_ref, widx_ref, od_ref, kidx_ref, ckv_ref, wuv_ref, wout_ref,
                o_ref, key_scr, *, topk):
    tq = C_QBLOCK
    kc = key_scr.shape[-1]
    i = pl.program_id(1)
    n_chunks = ((i + 1) * tq + kc - 1) // kc
    t_pos = i * tq + lax.broadcasted_iota(I32, (tq, 1), 0)

    qidx = qidx_ref[0]
    widx = widx_ref[0]

    def score_chunk(c, carry):
        kb = kidx_ref[0, pl.ds(pl.multiple_of(c * kc, kc), kc), :]
        score = jnp.zeros((tq, kc), F32)
        for hh in range(C_IDX_HEADS):
            logit = _dot_nt(qidx[:, hh * C_IDX_DIM:(hh + 1) * C_IDX_DIM], kb)
            score = score + jnp.maximum(logit, 0.0) * widx[:, hh:hh + 1]
        key_pos = c * kc + lax.broadcasted_iota(I32, (1, kc), 1)
        key_scr[c] = jnp.where(key_pos <= t_pos, _order_key(score), INT_MIN)
        return carry

    lax.fori_loop(0, n_chunks, score_chunk, 0)

    def count_ge(thr):
        def body(c, acc):
            return acc + jnp.where(key_scr[c] >= thr, 1.0, 0.0)
        acc = lax.fori_loop(0, n_chunks, body, jnp.zeros((tq, kc), F32))
        return jnp.sum(acc, axis=1, keepdims=True)

    def bit_step(it, prefix):
        trial = prefix | (jnp.int32(1) << (31 - it))
        cnt = count_ge(trial ^ INT_MIN)
        return jnp.where(cnt >= topk, trial, prefix)

    prefix = lax.fori_loop(0, 32, bit_step, jnp.zeros((tq, 1), I32))
    thr = jnp.maximum(prefix ^ INT_MIN, INT_MIN + 1)

    n_ge = count_ge(thr)
    has_surplus = jnp.logical_and(n_ge > topk, prefix != 0)

    @pl.when(jnp.max(jnp.where(has_surplus, 1.0, 0.0)) > 0.0)
    def _():
        n_gt = count_ge(thr + 1)
        need = jnp.where(has_surplus, topk - n_gt, float(kc) * 1e6)
        upper = (lax.broadcasted_iota(I32, (kc, kc), 0) <= lax.broadcasted_iota(I32, (kc, kc), 1))
        upper = jnp.where(upper, 1.0, 0.0).astype(BF16)

        def body(c, seen):
            keys = key_scr[c]
            eq = keys == thr
            rank = seen + _dot(jnp.where(eq, 1.0, 0.0).astype(BF16), upper)
            key_scr[c] = jnp.where(jnp.logical_and(eq, rank > need), INT_MIN, keys)
            return rank[:, kc - 1:kc]

        lax.fori_loop(0, n_chunks, body, jnp.zeros((tq, 1), F32))

    qlat = qlat_ref[0]
    qs = jnp.concatenate([qlat[:, hh * C_KV_RANK:(hh + 1) * C_KV_RANK] for hh in range(C_HEADS)], axis=0)
    rows = C_HEADS * tq

    def attn_chunk(c, carry):
        m, l, acc = carry
        kv = ckv_ref[0, pl.ds(pl.multiple_of(c * kc, kc), kc), :]
        sel = key_scr[c] >= thr
        sc = _dot_nt(qs, kv).reshape(C_HEADS, tq, kc)
        sc = jnp.where(sel[None], sc, -jnp.inf).reshape(rows, kc)
        m_new = jnp.maximum(m, jnp.max(sc, axis=1, keepdims=True))
        m_safe = jnp.where(m_new == -jnp.inf, 0.0, m_new)
        alpha = jnp.exp(m - m_safe)
        p = jnp.exp(sc - m_safe)
        l = alpha * l + jnp.sum(p, axis=1, keepdims=True)
        acc = alpha * acc + _dot(p.astype(BF16), kv)
        return m_new, l, acc

    init = (jnp.full((rows, 1), -jnp.inf, F32), jnp.zeros((rows, 1), F32), jnp.zeros((rows, C_KV_RANK), F32))
    _, l, acc = lax.fori_loop(0, n_chunks, attn_chunk, init)
    o_lat = (acc / l).astype(BF16)

    heads = [_dot(o_lat[hh * tq:(hh + 1) * tq, :], wuv_ref[hh]).astype(BF16) for hh in range(C_HEADS)]
    heads.append(od_ref[0])
    o_ref[0] = x_ref[0] + _dot(jnp.concatenate(heads, axis=-1), wout_ref[...])


def _dsa(x, q_lat, q_idx, w_idx, o_d, k_idx, ckv, w_uv, w_out):
    bn, s, d = x.shape
    tq = C_QBLOCK
    kc = min(DSA_KEY_CHUNK, s)
    topk = min(C_MAX_TOPK, s // 4)
    blk = lambda n: pl.BlockSpec((1, tq, n), lambda b, i: (b, i, 0))
    full = lambda n: pl.BlockSpec((1, s, n), lambda b, i: (b, 0, 0))
    return pl.pallas_call(
        functools.partial(_dsa_kernel, topk=float(topk)),
        grid=(bn, s // tq),
        in_specs=[blk(d), blk(q_lat.shape[-1]), blk(q_idx.shape[-1]), blk(w_idx.shape[-1]), blk(o_d.shape[-1]),
                  full(k_idx.shape[-1]), full(ckv.shape[-1]), _const_spec(w_uv.shape), _const_spec(w_out.shape)],
        out_specs=blk(d),
        out_shape=jax.ShapeDtypeStruct(x.shape, F32),
        scratch_shapes=[pltpu.VMEM((s // kc, tq, kc), I32)],
        compiler_params=_params(),
        name="dsa",
    )(x, q_lat, q_idx, w_idx, o_d, k_idx, ckv, w_uv.astype(BF16), w_out.astype(BF16))


def _row_tile(s, want):
    return want if s % want == 0 else s


def kernel(x, mem, hgrn_lower_bounds, even_mix_norm, even_w_in, even_a_out_norm, even_b_conv, even_w_out, odd_mix_norm, odd_w_in, odd_c_q_norm, odd_c_kv_norm, odd_c_w_uq, odd_c_w_uk, odd_c_w_uv, odd_c_idx_wq, odd_c_idx_k_g, odd_c_idx_k_b, odd_d_v_g, odd_d_v_b, odd_d_w_s, odd_d_b_s, odd_w_out, xa_norm, xa_mem_norm, xa_wq, xa_wkv, xa_wo, ffn_norm, ffn_w13, ffn_w2, final_norm):
    s = x.shape[1]
    tm = _row_tile(s, 512)
    k_mem, v_mem = _mem_kv(mem, xa_mem_norm, xa_wkv)

    def xattn_ffn(x, layer, last):
        return _xattn_ffn(x, k_mem[layer], v_mem[layer], xa_norm[layer], xa_wq[layer], xa_wo[layer],
                          ffn_norm[layer], ffn_w13[layer], ffn_w2[layer], final_norm,
                          final_norm=last, tm=tm)

    x = _even_mixer(x, hgrn_lower_bounds, even_mix_norm[0], even_w_in[0], even_a_out_norm[0],
                    even_b_conv[0], even_w_out[0], tm=tm)
    x = xattn_ffn(x, 0, False)
    q_lat, q_idx, w_idx, ckv, k_idx, o_d = _odd_proj(
        x, odd_mix_norm[0], odd_w_in[0], odd_c_q_norm[0], odd_c_kv_norm[0], odd_c_w_uq[0], odd_c_w_uk[0],
        odd_c_idx_wq[0], odd_c_idx_k_g[0], odd_c_idx_k_b[0], odd_d_v_g[0], odd_d_v_b[0],
        odd_d_w_s[0], odd_d_b_s[0], tm=tm)
    x = _dsa(x, q_lat, q_idx, w_idx, o_d, k_idx, ckv, odd_c_w_uv[0], odd_w_out[0])
    return xattn_ffn(x, 1, True)
```

```python
import functools

import jax
import jax.numpy as jnp
import numpy as np
from jax import lax
from jax.experimental import pallas as pl
from jax.experimental.pallas import tpu as pltpu

F32 = jnp.float32
BF16 = jnp.bfloat16
I32 = jnp.int32

V7X_VMEM_BYTES = 64 * 1024 * 1024
VMEM_LIMIT = 56 * 1024 * 1024
LANES = 128

A_HEADS, A_KDIM, A_VDIM, A_CHUNK = 4, 128, 128, 64
A_WIDTH = A_HEADS * A_VDIM
B_WIDTH = 512
C_HEADS, C_HEAD_DIM, C_Q_RANK, C_KV_RANK = 8, 64, 256, 128
C_IDX_HEADS, C_IDX_DIM, C_MAX_TOPK, C_QBLOCK = 8, 64, 256, 128
D_GROUPS, D_CHUNK, D_WIDTH = 4, 128, 512
X_HEADS = 4
HGRN_LEVELS = 6
DSA_KEY_CHUNK = 512
DSA_BITS_PER_CHECK = 4
INT_MIN = -(2 ** 31)


def _dot(a, b):
    return jnp.dot(a, b, preferred_element_type=F32)


def _dot_nt(a, b):
    return lax.dot_general(a, b, (((1,), (1,)), ((), ())), preferred_element_type=F32)


def _dot_tn(a, b):
    return lax.dot_general(a, b, (((0,), (0,)), ((), ())), preferred_element_type=F32)


def _rms(x, g, eps=1e-6):
    return x * lax.rsqrt(jnp.mean(x * x, axis=-1, keepdims=True) + eps) * g


def _layer_norm(x, g, b, eps=1e-5):
    mu = jnp.mean(x, axis=-1, keepdims=True)
    xc = x - mu
    var = jnp.mean(xc * xc, axis=-1, keepdims=True)
    return xc * lax.rsqrt(var + eps) * g + b


def _silu(x):
    return x * jax.nn.sigmoid(x)


def _gelu(x):
    return 0.5 * x * (1.0 + lax.erf(x * (2.0 ** -0.5)))


def _const_spec(shape):
    zeros = (0,) * len(shape)
    return pl.BlockSpec(shape, lambda *_: zeros, pipeline_mode=pl.Buffered(1))


def _params(n_parallel=0, n_arbitrary=2):
    return pltpu.CompilerParams(
        dimension_semantics=("parallel",) * n_parallel + ("arbitrary",) * n_arbitrary,
        vmem_limit_bytes=VMEM_LIMIT)


def _mem_kv_kernel(mem_ref, g_ref, wkv_ref, k_ref, v_ref):
    d = mem_ref.shape[-1]
    m = _rms(mem_ref[0], g_ref[0]).astype(BF16)
    kv = _dot(m, wkv_ref[0])
    k_ref[0, 0] = kv[:, :d].astype(BF16)
    v_ref[0, 0] = kv[:, d:].astype(BF16)


def _mem_kv(mem, mem_g, wkv):
    depth, d = mem_g.shape
    bn, m_len, _ = mem.shape
    out = jax.ShapeDtypeStruct((depth, bn, m_len, d), BF16)
    return pl.pallas_call(
        _mem_kv_kernel,
        grid=(depth, bn),
        in_specs=[pl.BlockSpec((1, m_len, d), lambda l, b: (b, 0, 0)),
                  pl.BlockSpec((1, 1, d), lambda l, b: (l, 0, 0)),
                  pl.BlockSpec((1, d, 2 * d), lambda l, b: (l, 0, 0))],
        out_specs=[pl.BlockSpec((1, 1, m_len, d), lambda l, b: (l, b, 0, 0))] * 2,
        out_shape=[out, out],
        compiler_params=_params(),
        name="mem_kv",
    )(mem, mem_g.reshape(depth, 1, d), wkv.astype(BF16))


def _xattn_ffn_kernel(*refs, final_norm, ff_chunk, mix_parts):
    n_mix = mix_parts + 1 if mix_parts else 0
    mix_refs, refs = refs[:n_mix], refs[n_mix:]
    x_ref, k_ref, v_ref, gxa_ref, wq_ref, wo_ref, gff_ref, w13_ref, w2_ref, gfin_ref, o_ref = refs
    x = x_ref[0]
    if mix_parts:
        mixed = jnp.concatenate([r[0] for r in mix_refs[:mix_parts]], axis=-1)
        x = x + _dot(mixed, mix_refs[mix_parts][...])
    d = x.shape[-1]
    hd = d // X_HEADS
    ff = w2_ref.shape[0]
    h = _rms(x, gxa_ref[...]).astype(BF16)
    q = _dot(h, wq_ref[...])
    heads = []
    for i in range(X_HEADS):
        cols = slice(i * hd, (i + 1) * hd)
        sc = _dot_nt(q[:, cols].astype(BF16), k_ref[0, :, cols]) * (hd ** -0.5)
        p = jnp.exp(sc - jnp.max(sc, axis=-1, keepdims=True))
        o = _dot(p.astype(BF16), v_ref[0, :, cols])
        heads.append((o / jnp.sum(p, axis=-1, keepdims=True)).astype(BF16))
    x = x + _dot(jnp.concatenate(heads, axis=-1), wo_ref[...])
    h2 = _rms(x, gff_ref[...]).astype(BF16)
    acc = jnp.zeros_like(x)
    for c0 in range(0, ff, ff_chunk):
        g = _dot(h2, w13_ref[:, c0:c0 + ff_chunk])
        u = _dot(h2, w13_ref[:, ff + c0:ff + c0 + ff_chunk])
        acc = acc + _dot((_silu(g) * u).astype(BF16), w2_ref[c0:c0 + ff_chunk, :])
    x = x + acc
    if final_norm:
        x = _rms(x, gfin_ref[...])
    o_ref[0] = x


def _xattn_ffn(x, k_mem, v_mem, g_xa, wq, wo, g_ff, w13, w2, g_fin, *, final_norm, tm, mix=(), w_mix=None):
    bn, s, d = x.shape
    mix_specs = [pl.BlockSpec((1, tm, a.shape[-1]), lambda b, i: (b, i, 0)) for a in mix]
    mix_args = list(mix)
    if mix:
        mix_specs.append(_const_spec(w_mix.shape))
        mix_args.append(w_mix.astype(BF16))
    m_len = k_mem.shape[1]
    ff = w2.shape[0]
    ff_chunk = ff // 2 if (ff // 2) % LANES == 0 else ff
    row = lambda a: a.reshape(1, d)
    kern = functools.partial(_xattn_ffn_kernel, final_norm=final_norm, ff_chunk=ff_chunk, mix_parts=len(mix))
    return pl.pallas_call(
        kern,
        grid=(bn, s // tm),
        in_specs=mix_specs + [
                  pl.BlockSpec((1, tm, d), lambda b, i: (b, i, 0)),
                  pl.BlockSpec((1, m_len, d), lambda b, i: (b, 0, 0)),
                  pl.BlockSpec((1, m_len, d), lambda b, i: (b, 0, 0)),
                  _const_spec((1, d)), _const_spec((d, d)), _const_spec((d, d)),
                  _const_spec((1, d)), _const_spec((d, 2 * ff)), _const_spec((ff, d)),
                  _const_spec((1, d))],
        out_specs=pl.BlockSpec((1, tm, d), lambda b, i: (b, i, 0)),
        out_shape=jax.ShapeDtypeStruct(x.shape, F32),
        compiler_params=_params(),
        name="xattn_ffn_final" if final_norm else "xattn_ffn",
    )(*mix_args, x, k_mem, v_mem, row(g_xa), wq.astype(BF16), wo.astype(BF16), row(g_ff),
      w13.astype(BF16), w2.astype(BF16), row(g_fin))


def _hgrn_tables():
    n = A_CHUNK
    t = np.arange(n)[:, None]
    u = np.arange(n)[None, :]
    sums, pair = [], []
    for lvl in range(HGRN_LEVELS):
        c = n >> (lvl + 1)
        mid = (t // (2 * c)) * (2 * c) + c
        late = t >= mid
        sums.append(np.where(late, (u >= mid) & (u <= t), (u > t) & (u < mid)))
        pair.append(late & (u < mid) & (u >= mid - c))
    sums.append(u <= t)
    sums.append(u > t)
    pair.append(t == u)
    return (np.concatenate(sums, axis=0).astype(np.float32),
            np.stack(pair, axis=0).astype(np.float32))


def _even_kernel(x_ref, lbp_ref, g_ref, win_ref, aon_ref, conv_ref, wout_ref, sums_ref, pair_ref,
                 o_ref, q_scr, k_scr, v_scr, lfh_scr, lfl_scr, oa_scr, z_scr, st_scr):
    tm = x_ref.shape[1]
    n = A_CHUNK

    @pl.when(pl.program_id(1) == 0)
    def _():
        st_scr[...] = jnp.zeros_like(st_scr)
        z_scr[0:8, :] = jnp.zeros((8, B_WIDTH), F32)

    x = x_ref[0]
    h = _rms(x, g_ref[...]).astype(BF16)

    def proj(j, width=A_WIDTH):
        return _dot(h, win_ref[:, j * width:(j + 1) * width])

    lbp = lbp_ref[...]
    e = jnp.exp(lbp - jnp.max(lbp, axis=0, keepdims=True))
    lb = e[0:1, :] / jnp.sum(e, axis=0, keepdims=True)

    q_scr[...] = _silu(proj(0))
    f = lb + (1.0 - lb) * jax.nn.sigmoid(proj(1))
    k_scr[...] = 1.0 - f
    logf = jnp.log(f)
    lf_hi = logf.astype(BF16)
    lfh_scr[...] = lf_hi
    lfl_scr[...] = (logf - lf_hi.astype(F32)).astype(BF16)
    v_scr[...] = proj(2)

    row = lax.broadcasted_iota(I32, (n, 1), 0)

    def chunk(c, carry):
        r0 = pl.multiple_of(c * n, n)
        rows = pl.ds(r0, n)
        sums = sums_ref[...]
        ex = jnp.exp(_dot(sums, lfh_scr[rows, :]) + _dot(sums, lfl_scr[rows, :]))
        qc, kc, vc = q_scr[rows, :], k_scr[rows, :], v_scr[rows, :]
        for hh in range(A_HEADS):
            cols = slice(hh * A_KDIM, (hh + 1) * A_KDIM)
            qh, kh = qc[:, cols], kc[:, cols]
            attn = _dot_nt(qh.astype(BF16), kh.astype(BF16)) * pair_ref[HGRN_LEVELS]
            for lvl in range(HGRN_LEVELS):
                late = (row & (n >> (lvl + 1))) != 0
                xl = (jnp.where(late, qh, kh) * ex[lvl * n:(lvl + 1) * n, cols]).astype(BF16)
                attn = attn + _dot_nt(xl, xl) * pair_ref[lvl]
            e_in = ex[HGRN_LEVELS * n:(HGRN_LEVELS + 1) * n, cols]
            e_out = ex[(HGRN_LEVELS + 1) * n:(HGRN_LEVELS + 2) * n, cols]
            st = st_scr[hh]
            vh = vc[:, cols].astype(BF16)
            o = _dot(attn.astype(BF16), vh) + _dot_nt((qh * e_in).astype(BF16), st.astype(BF16))
            oa_scr[rows, cols] = o
            st_scr[hh] = st * e_in[n - 1:n, :] + _dot_tn(vh, (kh * e_out).astype(BF16))
        return carry

    lax.fori_loop(0, tm // n, chunk, 0)

    gate = _silu(proj(3))
    parts = []
    for hh in range(A_HEADS):
        cols = slice(hh * A_VDIM, (hh + 1) * A_VDIM)
        parts.append((_rms(oa_scr[:, cols], aon_ref[hh:hh + 1, :]) * gate[:, cols]).astype(BF16))

    z_scr[8:8 + tm, :] = proj(5) * proj(6)
    w = conv_ref[...]
    y = (w[0:1, :] * z_scr[6:6 + tm, :] + w[1:2, :] * z_scr[7:7 + tm, :] + w[2:3, :] * z_scr[8:8 + tm, :])
    z_scr[0:8, :] = z_scr[tm:tm + 8, :]
    parts.append((proj(4) * y).astype(BF16))

    o_ref[0] = x + _dot(jnp.concatenate(parts, axis=-1), wout_ref[...])


def _even_mixer(x, lbp, g, w_in, a_out_norm, b_conv, w_out, *, tm):
    bn, s, d = x.shape
    sums, pair = _hgrn_tables()
    n_in = w_in.shape[1]
    return pl.pallas_call(
        _even_kernel,
        grid=(bn, s // tm),
        in_specs=[pl.BlockSpec((1, tm, d), lambda b, i: (b, i, 0)),
                  _const_spec(lbp.shape), _const_spec((1, d)), _const_spec((d, n_in)),
                  _const_spec(a_out_norm.shape), _const_spec(b_conv.shape),
                  _const_spec(w_out.shape), _const_spec(sums.shape), _const_spec(pair.shape)],
        out_specs=pl.BlockSpec((1, tm, d), lambda b, i: (b, i, 0)),
        out_shape=jax.ShapeDtypeStruct(x.shape, F32),
        scratch_shapes=[pltpu.VMEM((tm, A_WIDTH), F32), pltpu.VMEM((tm, A_WIDTH), F32),
                        pltpu.VMEM((tm, A_WIDTH), F32), pltpu.VMEM((tm, A_WIDTH), BF16),
                        pltpu.VMEM((tm, A_WIDTH), BF16), pltpu.VMEM((tm, A_WIDTH), F32),
                        pltpu.VMEM((tm + 8, B_WIDTH), F32),
                        pltpu.VMEM((A_HEADS, A_VDIM, A_KDIM), F32)],
        compiler_params=_params(),
        name="even_mixer",
    )(x, lbp, g.reshape(1, d), w_in.astype(BF16), a_out_norm, b_conv, w_out.astype(BF16),
      jnp.asarray(sums, BF16), jnp.asarray(pair, F32))


def _odd_proj_kernel(x_ref, g_ref, win_ref, wwt_ref, qg_ref, kvg_ref, wuq_ref, wukt_ref, iwqt_ref, ikg_ref, ikb_ref,
                     vg_ref, vb_ref, ws_ref, bs_ref,
                     qlat_t_ref, qidx_t_ref, widx_t_ref, ckv_ref, ckv_t_ref, kidx_ref, od_ref):
    tm = x_ref.shape[1]
    tq = C_QBLOCK
    kc = ckv_t_ref.shape[-1]
    h = _rms(x_ref[0], g_ref[...]).astype(BF16)
    o = 0

    def proj(width):
        nonlocal o
        r = _dot(h, win_ref[:, o:o + width])
        o += width
        return r

    cq = _rms(proj(C_Q_RANK), qg_ref[...]).astype(BF16)
    ckv = _rms(proj(C_KV_RANK), kvg_ref[...])
    ckv_ref[0] = ckv.astype(BF16)
    for j in range(tm // kc):
        ckv_t_ref[0, j] = ckv[j * kc:(j + 1) * kc, :].T.astype(BF16)
    du = proj(D_WIDTH)
    dv = proj(D_WIDTH)
    kidx_ref[0] = _layer_norm(proj(LANES)[:, :C_IDX_DIM], ikg_ref[...], ikb_ref[...]).astype(BF16)
    widx_t_ref[0] = _dot_nt(wwt_ref[...], h)[:C_IDX_HEADS, :] * (C_IDX_HEADS ** -0.5)

    q = _dot(cq, wuq_ref[...])
    for i in range(C_HEADS):
        qh = q[:, i * C_HEAD_DIM:(i + 1) * C_HEAD_DIM].astype(BF16)
        lat_t = (_dot_nt(wukt_ref[i], qh) * (C_HEAD_DIM ** -0.5)).astype(BF16)
        idx_t = (_dot_nt(iwqt_ref[i], cq) * (C_IDX_DIM ** -0.5)).astype(BF16)
        for j in range(tm // tq):
            qlat_t_ref[0, j, :, i * tq:(i + 1) * tq] = lat_t[:, j * tq:(j + 1) * tq]
            qidx_t_ref[0, j, :, i * tq:(i + 1) * tq] = idx_t[:, j * tq:(j + 1) * tq]

    u = _gelu(du)
    v = _layer_norm(_gelu(dv), vg_ref[...], vb_ref[...]).astype(BF16)
    tri = (lax.broadcasted_iota(I32, (D_CHUNK, D_CHUNK), 0) >= lax.broadcasted_iota(I32, (D_CHUNK, D_CHUNK), 1))
    gw = D_WIDTH // D_GROUPS
    for gi in range(D_GROUPS):
        wc = jnp.where(tri, ws_ref[gi], 0.0).astype(BF16)
        cols = slice(gi * gw, (gi + 1) * gw)
        for c0 in range(0, tm, D_CHUNK):
            mixed = _dot(wc, v[c0:c0 + D_CHUNK, cols]) + bs_ref[:, gi:gi + 1]
            od_ref[0, c0:c0 + D_CHUNK, cols] = (u[c0:c0 + D_CHUNK, cols] * mixed).astype(BF16)


def _odd_proj(x, g, w_in, q_g, kv_g, w_uq, w_uk, idx_wq, ik_g, ik_b, v_g, v_b, w_s, b_s, *, tm):
    bn, s, d = x.shape
    c0, c1, c2, c3 = C_Q_RANK, C_Q_RANK + C_KV_RANK, C_Q_RANK + C_KV_RANK + C_IDX_DIM, \
        C_Q_RANK + C_KV_RANK + C_IDX_DIM + C_IDX_HEADS
    pad = LANES - C_IDX_DIM
    w = jnp.concatenate([w_in[:, :c1], w_in[:, c3:], w_in[:, c1:c2], jnp.zeros((d, pad), w_in.dtype)],
                        axis=1).astype(BF16)
    ww_t = jnp.concatenate([w_in[:, c2:c3].T, jnp.zeros((16 - C_IDX_HEADS, d), w_in.dtype)], axis=0).astype(BF16)
    wuk_t = jnp.transpose(w_uk, (0, 2, 1)).astype(BF16)
    iwq_t = jnp.transpose(idx_wq.reshape(C_Q_RANK, C_IDX_HEADS, C_IDX_DIM), (1, 2, 0)).astype(BF16)
    kc = min(DSA_KEY_CHUNK, s)
    assert tm % kc == 0 and tm % C_QBLOCK == 0
    row = lambda a: a.reshape(1, -1)
    shp = lambda n, dt: jax.ShapeDtypeStruct((bn, s, n), dt)
    blk = lambda n: pl.BlockSpec((1, tm, n), lambda b, i: (b, i, 0))
    nq = s // C_QBLOCK
    heads_q = C_HEADS * C_QBLOCK
    qblk = lambda n: pl.BlockSpec((1, tm // C_QBLOCK, n, heads_q), lambda b, i: (b, i, 0, 0))
    return pl.pallas_call(
        _odd_proj_kernel,
        grid=(bn, s // tm),
        in_specs=[blk(d), _const_spec((1, d)), _const_spec(w.shape), _const_spec(ww_t.shape),
                  _const_spec((1, C_Q_RANK)), _const_spec((1, C_KV_RANK)),
                  _const_spec(w_uq.shape), _const_spec(wuk_t.shape), _const_spec(iwq_t.shape),
                  _const_spec((1, C_IDX_DIM)), _const_spec((1, C_IDX_DIM)),
                  _const_spec((1, D_WIDTH)), _const_spec((1, D_WIDTH)),
                  _const_spec(w_s.shape), _const_spec((D_CHUNK, D_GROUPS))],
        out_specs=[qblk(C_KV_RANK), qblk(C_IDX_DIM),
                   pl.BlockSpec((1, C_IDX_HEADS, tm), lambda b, i: (b, 0, i)),
                   blk(C_KV_RANK),
                   pl.BlockSpec((1, tm // kc, C_KV_RANK, kc), lambda b, i: (b, i, 0, 0)),
                   blk(C_IDX_DIM), blk(D_WIDTH)],
        out_shape=[jax.ShapeDtypeStruct((bn, nq, C_KV_RANK, heads_q), BF16),
                   jax.ShapeDtypeStruct((bn, nq, C_IDX_DIM, heads_q), BF16),
                   jax.ShapeDtypeStruct((bn, C_IDX_HEADS, s), F32),
                   shp(C_KV_RANK, BF16),
                   jax.ShapeDtypeStruct((bn, s // kc, C_KV_RANK, kc), BF16),
                   shp(C_IDX_DIM, BF16), shp(D_WIDTH, BF16)],
        compiler_params=_params(),
        name="odd_proj",
    )(x, row(g), w, ww_t, row(q_g), row(kv_g), w_uq.astype(BF16), wuk_t, iwq_t,
      row(ik_g), row(ik_b), row(v_g), row(v_b), w_s, b_s.T)


def _order_key(score):
    bits = lax.bitcast_convert_type(score + 0.0, I32)
    return bits ^ ((bits >> 31) & 0x7FFFFFFF)


def _dsa_kernel(qlat_t_ref, qidx_t_ref, widx_t_ref, kidx_ref, ckv_ref, ckv_t_ref, wuv_ref,
                o_ref, key_scr, s_scr, p_scr, m_scr, l_scr, a_scr, acc_scr, *, topk):
    tq = C_QBLOCK
    kc = key_scr.shape[1]
    i = pl.program_id(1)
    n_chunks = ((i + 1) * tq + kc - 1) // kc
    t_pos = i * tq + lax.broadcasted_iota(I32, (1, tq), 1)

    qidx_t = qidx_t_ref[0, 0]
    widx_t = widx_t_ref[0]

    def score_chunk(c, carry):
        kb = kidx_ref[0, pl.ds(pl.multiple_of(c * kc, kc), kc), :]
        logits = _dot(kb, qidx_t)
        score = jnp.zeros((kc, tq), F32)
        for hh in range(C_IDX_HEADS):
            score = score + jnp.maximum(logits[:, hh * tq:(hh + 1) * tq], 0.0) * widx_t[hh:hh + 1, :]
        key_pos = c * kc + lax.broadcasted_iota(I32, (kc, 1), 0)
        key_scr[c] = jnp.where(key_pos <= t_pos, _order_key(score), INT_MIN)
        return carry

    lax.fori_loop(0, n_chunks, score_chunk, 0)

    def count_ge(thr):
        fold = 4 * 8
        def body(c, acc):
            hit = jnp.where(key_scr[c] >= thr, 1.0, 0.0)
            return acc + jnp.sum(hit.reshape(kc // fold, fold, tq), axis=0)
        acc = lax.fori_loop(0, n_chunks, body, jnp.zeros((fold, tq), F32))
        return jnp.sum(acc, axis=0, keepdims=True)

    def unresolved(state):
        it, _, cnt = state
        return jnp.logical_and(it < 32, jnp.max(cnt) > topk)

    def bit_steps(state):
        it, prefix, cnt = state
        for _ in range(DSA_BITS_PER_CHECK):
            trial = prefix | (jnp.int32(1) << (31 - it))
            c = count_ge(trial ^ INT_MIN)
            keep = c >= topk
            it, prefix, cnt = it + 1, jnp.where(keep, trial, prefix), jnp.where(keep, c, cnt)
        return it, prefix, cnt

    n_causal = (t_pos + 1).astype(F32)
    _, prefix, n_ge = lax.while_loop(unresolved, bit_steps, (jnp.int32(0), jnp.zeros((1, tq), I32), n_causal))
    thr = jnp.maximum(prefix ^ INT_MIN, INT_MIN + 1)

    has_surplus = n_ge > topk

    @pl.when(jnp.max(n_ge) > topk)
    def _():
        n_gt = count_ge(thr + 1)
        need = jnp.where(has_surplus, topk - n_gt, float(kc) * 1e6)
        lower = (lax.broadcasted_iota(I32, (kc, kc), 0) >= lax.broadcasted_iota(I32, (kc, kc), 1))
        lower = jnp.where(lower, 1.0, 0.0).astype(BF16)

        def body(c, seen):
            keys = key_scr[c]
            eq = keys == thr
            rank = seen + _dot(lower, jnp.where(eq, 1.0, 0.0).astype(BF16))
            key_scr[c] = jnp.where(jnp.logical_and(eq, rank > need), INT_MIN, keys)
            return rank[kc - 1:kc, :]

        lax.fori_loop(0, n_chunks, body, jnp.zeros((1, tq), F32))

    qlat_t = qlat_t_ref[0, 0]
    m_scr[...] = jnp.full(m_scr.shape, -jnp.inf, F32)
    l_scr[...] = jnp.zeros(l_scr.shape, F32)
    acc_scr[...] = jnp.zeros(acc_scr.shape, F32)

    def attn_chunk(c, carry):
        kv = ckv_ref[0, pl.ds(pl.multiple_of(c * kc, kc), kc), :]
        s_scr[...] = _dot(kv, qlat_t)
        bias = jnp.where(key_scr[c] >= thr, 0.0, -jnp.inf)
        for hh in range(C_HEADS):
            cols = slice(hh * tq, (hh + 1) * tq)
            m_old = m_scr[:, cols]
            m_new = jnp.maximum(m_old, jnp.max(s_scr[:, cols] + bias, axis=0, keepdims=True))
            m_safe = jnp.where(m_new == -jnp.inf, 0.0, m_new)
            alpha = jnp.exp(m_old - m_safe)
            p = jnp.exp((s_scr[:, cols] - m_safe) + bias)
            l_scr[:, cols] = alpha * l_scr[:, cols] + jnp.sum(p, axis=0, keepdims=True)
            m_scr[:, cols] = m_new
            a_scr[:, cols] = alpha
            p_scr[:, cols] = p.astype(BF16)
        acc_scr[...] = a_scr[...] * acc_scr[...] + _dot(ckv_t_ref[0, c], p_scr[...])
        return carry

    lax.fori_loop(0, n_chunks, attn_chunk, 0)
    o_lat_t = acc_scr[...] / l_scr[...]
    o_ref[0] = jnp.concatenate(
        [_dot_tn(o_lat_t[:, hh * tq:(hh + 1) * tq].astype(BF16), wuv_ref[hh]).astype(BF16)
         for hh in range(C_HEADS)], axis=-1)


def _dsa(q_lat_t, q_idx_t, w_idx_t, k_idx, ckv, ckv_t, w_uv):
    bn, s, _ = ckv.shape
    tq = C_QBLOCK
    kc = ckv_t.shape[-1]
    topk = min(C_MAX_TOPK, s // 4)
    heads_q = C_HEADS * tq
    qblk = lambda n: pl.BlockSpec((1, 1, n, heads_q), lambda b, i: (b, i, 0, 0))
    full = lambda n: pl.BlockSpec((1, s, n), lambda b, i: (b, 0, 0))
    width = C_HEADS * C_HEAD_DIM
    return pl.pallas_call(
        functools.partial(_dsa_kernel, topk=float(topk)),
        grid=(bn, s // tq),
        in_specs=[qblk(C_KV_RANK), qblk(C_IDX_DIM),
                  pl.BlockSpec((1, C_IDX_HEADS, tq), lambda b, i: (b, 0, i)),
                  full(C_IDX_DIM), full(C_KV_RANK),
                  pl.BlockSpec((1, s // kc, C_KV_RANK, kc), lambda b, i: (b, 0, 0, 0)),
                  _const_spec(w_uv.shape)],
        out_specs=pl.BlockSpec((1, tq, width), lambda b, i: (b, i, 0)),
        out_shape=jax.ShapeDtypeStruct((bn, s, width), BF16),
        scratch_shapes=[pltpu.VMEM((s // kc, kc, tq), I32), pltpu.VMEM((kc, heads_q), F32),
                        pltpu.VMEM((kc, heads_q), BF16),
                        pltpu.VMEM((1, heads_q), F32), pltpu.VMEM((1, heads_q), F32), pltpu.VMEM((1, heads_q), F32),
                        pltpu.VMEM((C_KV_RANK, heads_q), F32)],
        compiler_params=_params(),
        name="dsa",
    )(q_lat_t, q_idx_t, w_idx_t, k_idx, ckv, ckv_t, w_uv.astype(BF16))


def _row_tile(s, want):
    return want if s % want == 0 else s


def kernel(x, mem, hgrn_lower_bounds, even_mix_norm, even_w_in, even_a_out_norm, even_b_conv, even_w_out, odd_mix_norm, odd_w_in, odd_c_q_norm, odd_c_kv_norm, odd_c_w_uq, odd_c_w_uk, odd_c_w_uv, odd_c_idx_wq, odd_c_idx_k_g, odd_c_idx_k_b, odd_d_v_g, odd_d_v_b, odd_d_w_s, odd_d_b_s, odd_w_out, xa_norm, xa_mem_norm, xa_wq, xa_wkv, xa_wo, ffn_norm, ffn_w13, ffn_w2, final_norm):
    s = x.shape[1]
    tm = _row_tile(s, 512)
    k_mem, v_mem = _mem_kv(mem, xa_mem_norm, xa_wkv)

    def xattn_ffn(x, layer, last, **mix):
        return _xattn_ffn(x, k_mem[layer], v_mem[layer], xa_norm[layer], xa_wq[layer], xa_wo[layer],
                          ffn_norm[layer], ffn_w13[layer], ffn_w2[layer], final_norm,
                          final_norm=last, tm=tm, **mix)

    x = _even_mixer(x, hgrn_lower_bounds, even_mix_norm[0], even_w_in[0], even_a_out_norm[0],
                    even_b_conv[0], even_w_out[0], tm=tm)
    x = xattn_ffn(x, 0, False)
    q_lat_t, q_idx_t, w_idx_t, ckv, ckv_t, k_idx, o_d = _odd_proj(
        x, odd_mix_norm[0], odd_w_in[0], odd_c_q_norm[0], odd_c_kv_norm[0], odd_c_w_uq[0], odd_c_w_uk[0],
        odd_c_idx_wq[0], odd_c_idx_k_g[0], odd_c_idx_k_b[0], odd_d_v_g[0], odd_d_v_b[0],
        odd_d_w_s[0], odd_d_b_s[0], tm=tm)
    o_c = _dsa(q_lat_t, q_idx_t, w_idx_t, k_idx, ckv, ckv_t, odd_c_w_uv[0])
    return xattn_ffn(x, 1, True, mix=(o_c, o_d), w_mix=odd_w_out[0])
```

```python
import functools

import jax
import jax.numpy as jnp
import numpy as np
from jax import lax
from jax.experimental import pallas as pl
from jax.experimental.pallas import tpu as pltpu

F32 = jnp.float32
BF16 = jnp.bfloat16
I32 = jnp.int32
I16 = jnp.int16

V7X_VMEM_BYTES = 64 * 1024 * 1024
VMEM_LIMIT = 56 * 1024 * 1024
LANES = 128

A_HEADS, A_KDIM, A_VDIM, A_CHUNK = 4, 128, 128, 64
A_WIDTH = A_HEADS * A_VDIM
B_WIDTH = 512
C_HEADS, C_HEAD_DIM, C_Q_RANK, C_KV_RANK = 8, 64, 256, 128
C_IDX_HEADS, C_IDX_DIM, C_MAX_TOPK, C_QBLOCK = 8, 64, 256, 128
D_GROUPS, D_CHUNK, D_WIDTH = 4, 128, 512
X_HEADS = 4
HGRN_LEVELS = 6
DSA_KEY_CHUNK = 512
DSA_BITS_PER_CHECK = 4
DSA_GROUP_HEADS = 2
LOG2E = 1.4426950408889634
INT16_MIN = -(2 ** 15)
INT_MIN = -(2 ** 31)


def _dot(a, b):
    return jnp.dot(a, b, preferred_element_type=F32)


def _dot_nt(a, b):
    return lax.dot_general(a, b, (((1,), (1,)), ((), ())), preferred_element_type=F32)


def _dot_tn(a, b):
    return lax.dot_general(a, b, (((0,), (0,)), ((), ())), preferred_element_type=F32)


def _rms(x, g, eps=1e-6):
    return x * lax.rsqrt(jnp.mean(x * x, axis=-1, keepdims=True) + eps) * g


def _layer_norm(x, g, b, eps=1e-5):
    mu = jnp.mean(x, axis=-1, keepdims=True)
    xc = x - mu
    var = jnp.mean(xc * xc, axis=-1, keepdims=True)
    return xc * lax.rsqrt(var + eps) * g + b


def _silu(x):
    return x * jax.nn.sigmoid(x)


def _gelu(x):
    return 0.5 * x * (1.0 + lax.erf(x * (2.0 ** -0.5)))


def _const_spec(shape):
    zeros = (0,) * len(shape)
    return pl.BlockSpec(shape, lambda *_: zeros, pipeline_mode=pl.Buffered(1))


def _params(n_parallel=0, n_arbitrary=2):
    return pltpu.CompilerParams(
        dimension_semantics=("parallel",) * n_parallel + ("arbitrary",) * n_arbitrary,
        vmem_limit_bytes=VMEM_LIMIT)


def _mem_kv_kernel(mem_ref, g_ref, wkv_ref, k_ref, v_ref):
    d = mem_ref.shape[-1]
    m = _rms(mem_ref[0], g_ref[0]).astype(BF16)
    kv = _dot(m, wkv_ref[0])
    k_ref[0, 0] = kv[:, :d].astype(BF16)
    v_ref[0, 0] = kv[:, d:].astype(BF16)


def _mem_kv(mem, mem_g, wkv):
    depth, d = mem_g.shape
    bn, m_len, _ = mem.shape
    out = jax.ShapeDtypeStruct((depth, bn, m_len, d), BF16)
    return pl.pallas_call(
        _mem_kv_kernel,
        grid=(depth, bn),
        in_specs=[pl.BlockSpec((1, m_len, d), lambda l, b: (b, 0, 0)),
                  pl.BlockSpec((1, 1, d), lambda l, b: (l, 0, 0)),
                  pl.BlockSpec((1, d, 2 * d), lambda l, b: (l, 0, 0))],
        out_specs=[pl.BlockSpec((1, 1, m_len, d), lambda l, b: (l, b, 0, 0))] * 2,
        out_shape=[out, out],
        compiler_params=_params(),
        name="mem_kv",
    )(mem, mem_g.reshape(depth, 1, d), wkv.astype(BF16))


def _xattn_ffn_kernel(*refs, final_norm, ff_chunk, mix_parts):
    n_mix = mix_parts + 1 if mix_parts else 0
    mix_refs, refs = refs[:n_mix], refs[n_mix:]
    x_ref, k_ref, v_ref, gxa_ref, wq_ref, wo_ref, gff_ref, w13_ref, w2_ref, gfin_ref, o_ref = refs
    x = x_ref[0]
    if mix_parts:
        mixed = jnp.concatenate([r[0] for r in mix_refs[:mix_parts]], axis=-1)
        x = x + _dot(mixed, mix_refs[mix_parts][...])
    d = x.shape[-1]
    hd = d // X_HEADS
    ff = w2_ref.shape[0]
    h = _rms(x, gxa_ref[...]).astype(BF16)
    q = _dot(h, wq_ref[...])
    heads = []
    for i in range(X_HEADS):
        cols = slice(i * hd, (i + 1) * hd)
        sc = _dot_nt(q[:, cols].astype(BF16), k_ref[0, :, cols]) * (hd ** -0.5)
        p = jnp.exp(sc - jnp.max(sc, axis=-1, keepdims=True))
        o = _dot(p.astype(BF16), v_ref[0, :, cols])
        heads.append((o / jnp.sum(p, axis=-1, keepdims=True)).astype(BF16))
    x = x + _dot(jnp.concatenate(heads, axis=-1), wo_ref[...])
    h2 = _rms(x, gff_ref[...]).astype(BF16)
    acc = jnp.zeros_like(x)
    for c0 in range(0, ff, ff_chunk):
        g = _dot(h2, w13_ref[:, c0:c0 + ff_chunk])
        u = _dot(h2, w13_ref[:, ff + c0:ff + c0 + ff_chunk])
        acc = acc + _dot((_silu(g) * u).astype(BF16), w2_ref[c0:c0 + ff_chunk, :])
    x = x + acc
    if final_norm:
        x = _rms(x, gfin_ref[...])
    o_ref[0] = x


def _xattn_ffn(x, k_mem, v_mem, g_xa, wq, wo, g_ff, w13, w2, g_fin, *, final_norm, tm, mix=(), w_mix=None):
    bn, s, d = x.shape
    mix_specs = [pl.BlockSpec((1, tm, a.shape[-1]), lambda b, i: (b, i, 0)) for a in mix]
    mix_args = list(mix)
    if mix:
        mix_specs.append(_const_spec(w_mix.shape))
        mix_args.append(w_mix.astype(BF16))
    m_len = k_mem.shape[1]
    ff = w2.shape[0]
    ff_chunk = ff // 2 if (ff // 2) % LANES == 0 else ff
    row = lambda a: a.reshape(1, d)
    kern = functools.partial(_xattn_ffn_kernel, final_norm=final_norm, ff_chunk=ff_chunk, mix_parts=len(mix))
    return pl.pallas_call(
        kern,
        grid=(bn, s // tm),
        in_specs=mix_specs + [
                  pl.BlockSpec((1, tm, d), lambda b, i: (b, i, 0)),
                  pl.BlockSpec((1, m_len, d), lambda b, i: (b, 0, 0)),
                  pl.BlockSpec((1, m_len, d), lambda b, i: (b, 0, 0)),
                  _const_spec((1, d)), _const_spec((d, d)), _const_spec((d, d)),
                  _const_spec((1, d)), _const_spec((d, 2 * ff)), _const_spec((ff, d)),
                  _const_spec((1, d))],
        out_specs=pl.BlockSpec((1, tm, d), lambda b, i: (b, i, 0)),
        out_shape=jax.ShapeDtypeStruct(x.shape, F32),
        compiler_params=_params(),
        name="xattn_ffn_final" if final_norm else "xattn_ffn",
    )(*mix_args, x, k_mem, v_mem, row(g_xa), wq.astype(BF16), wo.astype(BF16), row(g_ff),
      w13.astype(BF16), w2.astype(BF16), row(g_fin))


def _hgrn_tables():
    n = A_CHUNK
    t = np.arange(n)[:, None]
    u = np.arange(n)[None, :]
    sums, pair = [], []
    for lvl in range(HGRN_LEVELS):
        c = n >> (lvl + 1)
        mid = (t // (2 * c)) * (2 * c) + c
        late = t >= mid
        sums.append(np.where(late, (u >= mid) & (u <= t), (u > t) & (u < mid)))
        pair.append(late & (u < mid) & (u >= mid - c))
    sums.append(u <= t)
    sums.append(u > t)
    pair.append(t == u)
    return (np.concatenate(sums, axis=0).astype(np.float32),
            np.stack(pair, axis=0).astype(np.float32))


def _even_kernel(x_ref, lbp_ref, g_ref, win_ref, aon_ref, conv_ref, wout_ref, sums_ref, pair_ref,
                 o_ref, q_scr, k_scr, v_scr, lfh_scr, lfl_scr, oa_scr, z_scr, st_scr):
    tm = x_ref.shape[1]
    n = A_CHUNK

    @pl.when(pl.program_id(1) == 0)
    def _():
        st_scr[...] = jnp.zeros_like(st_scr)
        z_scr[0:8, :] = jnp.zeros((8, B_WIDTH), F32)

    x = x_ref[0]
    h = _rms(x, g_ref[...]).astype(BF16)

    def proj(j, width=A_WIDTH):
        return _dot(h, win_ref[:, j * width:(j + 1) * width])

    lbp = lbp_ref[...]
    e = jnp.exp(lbp - jnp.max(lbp, axis=0, keepdims=True))
    lb = e[0:1, :] / jnp.sum(e, axis=0, keepdims=True)

    q_scr[...] = _silu(proj(0))
    f = lb + (1.0 - lb) * jax.nn.sigmoid(proj(1))
    k_scr[...] = 1.0 - f
    logf = jnp.log(f)
    lf_hi = logf.astype(BF16)
    lfh_scr[...] = lf_hi
    lfl_scr[...] = (logf - lf_hi.astype(F32)).astype(BF16)
    v_scr[...] = proj(2)

    row = lax.broadcasted_iota(I32, (n, 1), 0)

    def chunk(c, carry):
        r0 = pl.multiple_of(c * n, n)
        rows = pl.ds(r0, n)
        sums = sums_ref[...]
        ex = jnp.exp(_dot(sums, lfh_scr[rows, :]) + _dot(sums, lfl_scr[rows, :]))
        qc, kc, vc = q_scr[rows, :], k_scr[rows, :], v_scr[rows, :]
        for hh in range(A_HEADS):
            cols = slice(hh * A_KDIM, (hh + 1) * A_KDIM)
            qh, kh = qc[:, cols], kc[:, cols]
            attn = _dot_nt(qh.astype(BF16), kh.astype(BF16)) * pair_ref[HGRN_LEVELS]
            for lvl in range(HGRN_LEVELS):
                late = (row & (n >> (lvl + 1))) != 0
                xl = (jnp.where(late, qh, kh) * ex[lvl * n:(lvl + 1) * n, cols]).astype(BF16)
                attn = attn + _dot_nt(xl, xl) * pair_ref[lvl]
            e_in = ex[HGRN_LEVELS * n:(HGRN_LEVELS + 1) * n, cols]
            e_out = ex[(HGRN_LEVELS + 1) * n:(HGRN_LEVELS + 2) * n, cols]
            st = st_scr[hh]
            vh = vc[:, cols].astype(BF16)
            o = _dot(attn.astype(BF16), vh) + _dot_nt((qh * e_in).astype(BF16), st.astype(BF16))
            oa_scr[rows, cols] = o
            st_scr[hh] = st * e_in[n - 1:n, :] + _dot_tn(vh, (kh * e_out).astype(BF16))
        return carry

    lax.fori_loop(0, tm // n, chunk, 0, unroll=2)

    gate = _silu(proj(3))
    parts = []
    for hh in range(A_HEADS):
        cols = slice(hh * A_VDIM, (hh + 1) * A_VDIM)
        parts.append((_rms(oa_scr[:, cols], aon_ref[hh:hh + 1, :]) * gate[:, cols]).astype(BF16))

    z_scr[8:8 + tm, :] = proj(5) * proj(6)
    w = conv_ref[...]
    y = (w[0:1, :] * z_scr[6:6 + tm, :] + w[1:2, :] * z_scr[7:7 + tm, :] + w[2:3, :] * z_scr[8:8 + tm, :])
    z_scr[0:8, :] = z_scr[tm:tm + 8, :]
    parts.append((proj(4) * y).astype(BF16))

    o_ref[0] = x + _dot(jnp.concatenate(parts, axis=-1), wout_ref[...])


def _even_mixer(x, lbp, g, w_in, a_out_norm, b_conv, w_out, *, tm):
    bn, s, d = x.shape
    sums, pair = _hgrn_tables()
    n_in = w_in.shape[1]
    return pl.pallas_call(
        _even_kernel,
        grid=(bn, s // tm),
        in_specs=[pl.BlockSpec((1, tm, d), lambda b, i: (b, i, 0)),
                  _const_spec(lbp.shape), _const_spec((1, d)), _const_spec((d, n_in)),
                  _const_spec(a_out_norm.shape), _const_spec(b_conv.shape),
                  _const_spec(w_out.shape), _const_spec(sums.shape), _const_spec(pair.shape)],
        out_specs=pl.BlockSpec((1, tm, d), lambda b, i: (b, i, 0)),
        out_shape=jax.ShapeDtypeStruct(x.shape, F32),
        scratch_shapes=[pltpu.VMEM((tm, A_WIDTH), F32), pltpu.VMEM((tm, A_WIDTH), F32),
                        pltpu.VMEM((tm, A_WIDTH), F32), pltpu.VMEM((tm, A_WIDTH), BF16),
                        pltpu.VMEM((tm, A_WIDTH), BF16), pltpu.VMEM((tm, A_WIDTH), F32),
                        pltpu.VMEM((tm + 8, B_WIDTH), F32),
                        pltpu.VMEM((A_HEADS, A_VDIM, A_KDIM), F32)],
        compiler_params=_params(),
        name="even_mixer",
    )(x, lbp, g.reshape(1, d), w_in.astype(BF16), a_out_norm, b_conv, w_out.astype(BF16),
      jnp.asarray(sums, BF16), jnp.asarray(pair, F32))


def _odd_proj_kernel(x_ref, g_ref, win_ref, wwt_ref, qg_ref, kvg_ref, wuq_ref, wukt_ref, iwqt_ref, ikg_ref, ikb_ref,
                     vg_ref, vb_ref, ws_ref, bs_ref,
                     qlat_t_ref, qidx_t_ref, widx_t_ref, ckv_ref, ckv_t_ref, kidx_ref, od_ref):
    tm = x_ref.shape[1]
    tq = C_QBLOCK
    kc = ckv_t_ref.shape[-1]
    h = _rms(x_ref[0], g_ref[...]).astype(BF16)
    o = 0

    def proj(width):
        nonlocal o
        r = _dot(h, win_ref[:, o:o + width])
        o += width
        return r

    cq = _rms(proj(C_Q_RANK), qg_ref[...]).astype(BF16)
    ckv = _rms(proj(C_KV_RANK), kvg_ref[...])
    ckv_ref[0] = ckv.astype(BF16)
    for j in range(tm // kc):
        ckv_t_ref[0, j] = ckv[j * kc:(j + 1) * kc, :].T.astype(BF16)
    du = proj(D_WIDTH)
    dv = proj(D_WIDTH)
    kidx_ref[0] = _layer_norm(proj(LANES)[:, :C_IDX_DIM], ikg_ref[...], ikb_ref[...]).astype(BF16)
    widx_t_ref[0] = _dot_nt(wwt_ref[...], h)[:C_IDX_HEADS, :] * (C_IDX_HEADS ** -0.5)

    q = _dot(cq, wuq_ref[...])
    for i in range(C_HEADS):
        qh = q[:, i * C_HEAD_DIM:(i + 1) * C_HEAD_DIM].astype(BF16)
        lat_t = (_dot_nt(wukt_ref[i], qh) * (C_HEAD_DIM ** -0.5 * LOG2E)).astype(BF16)
        idx_t = (_dot_nt(iwqt_ref[i], cq) * (C_IDX_DIM ** -0.5)).astype(BF16)
        g, k = divmod(i, DSA_GROUP_HEADS)
        for j in range(tm // tq):
            qlat_t_ref[0, j, g, :, k * tq:(k + 1) * tq] = lat_t[:, j * tq:(j + 1) * tq]
            qidx_t_ref[0, j, :, i * tq:(i + 1) * tq] = idx_t[:, j * tq:(j + 1) * tq]

    u = _gelu(du)
    v = _layer_norm(_gelu(dv), vg_ref[...], vb_ref[...]).astype(BF16)
    tri = (lax.broadcasted_iota(I32, (D_CHUNK, D_CHUNK), 0) >= lax.broadcasted_iota(I32, (D_CHUNK, D_CHUNK), 1))
    gw = D_WIDTH // D_GROUPS
    for gi in range(D_GROUPS):
        wc = jnp.where(tri, ws_ref[gi], 0.0).astype(BF16)
        cols = slice(gi * gw, (gi + 1) * gw)
        for c0 in range(0, tm, D_CHUNK):
            mixed = _dot(wc, v[c0:c0 + D_CHUNK, cols]) + bs_ref[:, gi:gi + 1]
            od_ref[0, c0:c0 + D_CHUNK, cols] = (u[c0:c0 + D_CHUNK, cols] * mixed).astype(BF16)


def _odd_proj(x, g, w_in, q_g, kv_g, w_uq, w_uk, idx_wq, ik_g, ik_b, v_g, v_b, w_s, b_s, *, tm):
    bn, s, d = x.shape
    c0, c1, c2, c3 = C_Q_RANK, C_Q_RANK + C_KV_RANK, C_Q_RANK + C_KV_RANK + C_IDX_DIM, \
        C_Q_RANK + C_KV_RANK + C_IDX_DIM + C_IDX_HEADS
    pad = LANES - C_IDX_DIM
    w = jnp.concatenate([w_in[:, :c1], w_in[:, c3:], w_in[:, c1:c2], jnp.zeros((d, pad), w_in.dtype)],
                        axis=1).astype(BF16)
    ww_t = jnp.concatenate([w_in[:, c2:c3].T, jnp.zeros((16 - C_IDX_HEADS, d), w_in.dtype)], axis=0).astype(BF16)
    wuk_t = jnp.transpose(w_uk, (0, 2, 1)).astype(BF16)
    iwq_t = jnp.transpose(idx_wq.reshape(C_Q_RANK, C_IDX_HEADS, C_IDX_DIM), (1, 2, 0)).astype(BF16)
    kc = min(DSA_KEY_CHUNK, s)
    assert tm % kc == 0 and tm % C_QBLOCK == 0
    row = lambda a: a.reshape(1, -1)
    shp = lambda n, dt: jax.ShapeDtypeStruct((bn, s, n), dt)
    blk = lambda n: pl.BlockSpec((1, tm, n), lambda b, i: (b, i, 0))
    nq = s // C_QBLOCK
    heads_q = C_HEADS * C_QBLOCK
    n_groups, group_q = C_HEADS // DSA_GROUP_HEADS, DSA_GROUP_HEADS * C_QBLOCK
    qblk = lambda n: pl.BlockSpec((1, tm // C_QBLOCK, n, heads_q), lambda b, i: (b, i, 0, 0))
    return pl.pallas_call(
        _odd_proj_kernel,
        grid=(bn, s // tm),
        in_specs=[blk(d), _const_spec((1, d)), _const_spec(w.shape), _const_spec(ww_t.shape),
                  _const_spec((1, C_Q_RANK)), _const_spec((1, C_KV_RANK)),
                  _const_spec(w_uq.shape), _const_spec(wuk_t.shape), _const_spec(iwq_t.shape),
                  _const_spec((1, C_IDX_DIM)), _const_spec((1, C_IDX_DIM)),
                  _const_spec((1, D_WIDTH)), _const_spec((1, D_WIDTH)),
                  _const_spec(w_s.shape), _const_spec((D_CHUNK, D_GROUPS))],
        out_specs=[pl.BlockSpec((1, tm // C_QBLOCK, n_groups, C_KV_RANK, group_q), lambda b, i: (b, i, 0, 0, 0)),
                   qblk(C_IDX_DIM),
                   pl.BlockSpec((1, C_IDX_HEADS, tm), lambda b, i: (b, 0, i)),
                   blk(C_KV_RANK),
                   pl.BlockSpec((1, tm // kc, C_KV_RANK, kc), lambda b, i: (b, i, 0, 0)),
                   blk(C_IDX_DIM), blk(D_WIDTH)],
        out_shape=[jax.ShapeDtypeStruct((bn, nq, n_groups, C_KV_RANK, group_q), BF16),
                   jax.ShapeDtypeStruct((bn, nq, C_IDX_DIM, heads_q), BF16),
                   jax.ShapeDtypeStruct((bn, C_IDX_HEADS, s), F32),
                   shp(C_KV_RANK, BF16),
                   jax.ShapeDtypeStruct((bn, s // kc, C_KV_RANK, kc), BF16),
                   shp(C_IDX_DIM, BF16), shp(D_WIDTH, BF16)],
        compiler_params=_params(),
        name="odd_proj",
    )(x, row(g), w, ww_t, row(q_g), row(kv_g), w_uq.astype(BF16), wuk_t, iwq_t,
      row(ik_g), row(ik_b), row(v_g), row(v_b), w_s, b_s.T)


def _order_key(score):
    bits = lax.bitcast_convert_type(score + 0.0, I32)
    return bits ^ ((bits >> 31) & 0x7FFFFFFF)


def _dsa_kernel(qlat_t_ref, qidx_t_ref, widx_t_ref, kidx_ref, ckv_ref, ckv_t_ref, wuv_ref,
                o_ref, key_scr, hi_scr, lo_scr, s_scr, mx_scr, m_scr, l_scr, acc_scr, *, topk):
    tq = C_QBLOCK
    kc = key_scr.shape[1]
    i = pl.program_id(1)
    n_chunks = ((i + 1) * tq + kc - 1) // kc
    t_pos = i * tq + lax.broadcasted_iota(I32, (1, tq), 1)

    qidx_t = qidx_t_ref[0, 0]
    widx_t = widx_t_ref[0]

    def score_chunk(c, carry):
        kb = kidx_ref[0, pl.ds(pl.multiple_of(c * kc, kc), kc), :]
        logits = _dot(kb, qidx_t)
        score = jnp.zeros((kc, tq), F32)
        for hh in range(C_IDX_HEADS):
            score = score + jnp.maximum(logits[:, hh * tq:(hh + 1) * tq], 0.0) * widx_t[hh:hh + 1, :]
        key_pos = c * kc + lax.broadcasted_iota(I32, (kc, 1), 0)
        key = jnp.where(key_pos <= t_pos, _order_key(score), INT_MIN)
        key_scr[c] = key
        hi_scr[c] = (key >> 16).astype(I16)
        lo_scr[c] = ((key & 0xFFFF) + INT16_MIN).astype(I16)
        return carry

    lax.fori_loop(0, n_chunks, score_chunk, 0)

    def count16(ref, thr16, strict=False):
        fold = 4 * 16
        def body(c, acc):
            x = ref[c]
            hit = jnp.where((x > thr16) if strict else (x >= thr16), jnp.int16(1), jnp.int16(0))
            hit = hit.reshape(kc // fold, fold, tq)
            for r in range(kc // fold):
                acc = acc + hit[r]
            return acc
        acc = lax.fori_loop(0, n_chunks, body, jnp.zeros((fold, tq), I16))
        return jnp.sum(acc.astype(I32), axis=0, keepdims=True).astype(F32)

    def to_i16(prefix):
        return (prefix + INT16_MIN).astype(I16)

    def half_search(ref, n_above, cnt):
        def unresolved(state):
            it, _, cnt = state
            return jnp.logical_and(it < 16, jnp.max(cnt) > topk)

        def bit_steps(state):
            it, prefix, cnt = state
            for _ in range(DSA_BITS_PER_CHECK):
                trial = prefix | (jnp.int32(1) << (15 - it))
                c = n_above + count16(ref, to_i16(trial))
                keep = c >= topk
                it, prefix, cnt = it + 1, jnp.where(keep, trial, prefix), jnp.where(keep, c, cnt)
            return it, prefix, cnt

        _, prefix, cnt = lax.while_loop(unresolved, bit_steps, (jnp.int32(0), jnp.zeros((1, tq), I32), cnt))
        return prefix, cnt

    n_causal = (t_pos + 1).astype(F32)
    hi_prefix, n_ge = half_search(hi_scr, 0.0, n_causal)
    hi_thr = to_i16(hi_prefix)
    n_gt_hi = count16(hi_scr, hi_thr, strict=True)

    def keep_matching_low(c, carry):
        lo_scr[c] = jnp.where(hi_scr[c] == hi_thr, lo_scr[c], jnp.int16(INT16_MIN))
        return carry

    lax.fori_loop(0, n_chunks, keep_matching_low, 0)
    lo_prefix, n_ge = half_search(lo_scr, n_gt_hi, n_ge)
    prefix = (hi_prefix << 16) | lo_prefix
    thr = jnp.maximum(prefix ^ INT_MIN, INT_MIN + 1)

    def count_ge(thr):
        fold = 4 * 8
        def body(c, acc):
            hit = jnp.where(key_scr[c] >= thr, 1.0, 0.0)
            return acc + jnp.sum(hit.reshape(kc // fold, fold, tq), axis=0)
        acc = lax.fori_loop(0, n_chunks, body, jnp.zeros((fold, tq), F32))
        return jnp.sum(acc, axis=0, keepdims=True)

    has_surplus = n_ge > topk

    @pl.when(jnp.max(n_ge) > topk)
    def _():
        n_gt = count_ge(thr + 1)
        need = jnp.where(has_surplus, topk - n_gt, float(kc) * 1e6)
        lower = (lax.broadcasted_iota(I32, (kc, kc), 0) >= lax.broadcasted_iota(I32, (kc, kc), 1))
        lower = jnp.where(lower, 1.0, 0.0).astype(BF16)

        def body(c, seen):
            keys = key_scr[c]
            eq = keys == thr
            rank = seen + _dot(lower, jnp.where(eq, 1.0, 0.0).astype(BF16))
            key_scr[c] = jnp.where(jnp.logical_and(eq, rank > need), INT_MIN, keys)
            return rank[kc - 1:kc, :]

        lax.fori_loop(0, n_chunks, body, jnp.zeros((1, tq), F32))

    n_groups = s_scr.shape[1]
    m_scr[...] = jnp.full(m_scr.shape, -jnp.inf, F32)
    l_scr[...] = jnp.zeros(l_scr.shape, F32)
    acc_scr[...] = jnp.zeros(acc_scr.shape, F32)

    def pipeline_step(c_next, c_now):
        if c_next is not None:
            c_next, slot_next = c_next
            kv = ckv_ref[0, pl.ds(pl.multiple_of(c_next * kc, kc), kc), :]
            bias = jnp.where(key_scr[c_next] >= thr, 0.0, -jnp.inf)
            bias = jnp.concatenate([bias] * DSA_GROUP_HEADS, axis=1)
        if c_now is not None:
            c_now, slot_now = c_now
            kv_t = ckv_t_ref[0, c_now]
        for g in range(n_groups):
            if c_next is not None:
                sb = _dot(kv, qlat_t_ref[0, 0, g]) + bias
                s_scr[slot_next, g] = sb
                mx_scr[slot_next, g] = jnp.max(sb, axis=0, keepdims=True)
            if c_now is not None:
                m_old = m_scr[g]
                m_new = jnp.maximum(m_old, mx_scr[slot_now, g])
                m_safe = jnp.where(m_new == -jnp.inf, 0.0, m_new)
                alpha = jnp.exp2(m_old - m_safe)
                p = jnp.exp2(s_scr[slot_now, g] - m_safe)
                l_scr[g] = alpha * l_scr[g] + jnp.sum(p, axis=0, keepdims=True)
                m_scr[g] = m_new
                acc_scr[g] = alpha * acc_scr[g] + _dot(kv_t, p.astype(BF16))

    pipeline_step((jnp.int32(0), 0), None)
    n_pairs = (n_chunks - 1) // 2

    def attn_pair(j, carry):
        c = 2 * j
        pipeline_step((c + 1, 1), (c, 0))
        pipeline_step((c + 2, 0), (c + 1, 1))
        return carry

    lax.fori_loop(0, n_pairs, attn_pair, 0)
    last = n_chunks - 1

    @pl.when(last == 2 * n_pairs)
    def _():
        pipeline_step(None, (last, 0))

    @pl.when(last != 2 * n_pairs)
    def _():
        pipeline_step((last, 1), (last - 1, 0))
        pipeline_step(None, (last, 1))
    outs = []
    for hh in range(C_HEADS):
        g, k = divmod(hh, DSA_GROUP_HEADS)
        o_lat_t = (acc_scr[g] / l_scr[g])[:, k * tq:(k + 1) * tq]
        outs.append(_dot_tn(o_lat_t.astype(BF16), wuv_ref[hh]).astype(BF16))
    o_ref[0] = jnp.concatenate(outs, axis=-1)


def _dsa(q_lat_t, q_idx_t, w_idx_t, k_idx, ckv, ckv_t, w_uv):
    bn, s, _ = ckv.shape
    tq = C_QBLOCK
    kc = ckv_t.shape[-1]
    topk = min(C_MAX_TOPK, s // 4)
    heads_q = C_HEADS * tq
    n_groups, group_q = q_lat_t.shape[2], q_lat_t.shape[4]
    full = lambda n: pl.BlockSpec((1, s, n), lambda b, i: (b, 0, 0))
    width = C_HEADS * C_HEAD_DIM
    return pl.pallas_call(
        functools.partial(_dsa_kernel, topk=float(topk)),
        grid=(bn, s // tq),
        in_specs=[pl.BlockSpec((1, 1, n_groups, C_KV_RANK, group_q), lambda b, i: (b, i, 0, 0, 0)),
                  pl.BlockSpec((1, 1, C_IDX_DIM, heads_q), lambda b, i: (b, i, 0, 0)),
                  pl.BlockSpec((1, C_IDX_HEADS, tq), lambda b, i: (b, 0, i)),
                  full(C_IDX_DIM), full(C_KV_RANK),
                  pl.BlockSpec((1, s // kc, C_KV_RANK, kc), lambda b, i: (b, 0, 0, 0)),
                  _const_spec(w_uv.shape)],
        out_specs=pl.BlockSpec((1, tq, width), lambda b, i: (b, i, 0)),
        out_shape=jax.ShapeDtypeStruct((bn, s, width), BF16),
        scratch_shapes=[pltpu.VMEM((s // kc, kc, tq), I32), pltpu.VMEM((s // kc, kc, tq), I16),
                        pltpu.VMEM((s // kc, kc, tq), I16), pltpu.VMEM((2, n_groups, kc, group_q), F32),
                        pltpu.VMEM((2, n_groups, 1, group_q), F32),
                        pltpu.VMEM((n_groups, 1, group_q), F32), pltpu.VMEM((n_groups, 1, group_q), F32),
                        pltpu.VMEM((n_groups, C_KV_RANK, group_q), F32)],
        compiler_params=_params(),
        name="dsa",
    )(q_lat_t, q_idx_t, w_idx_t, k_idx, ckv, ckv_t, w_uv.astype(BF16))


def _row_tile(s, want):
    return want if s % want == 0 else s


def kernel(x, mem, hgrn_lower_bounds, even_mix_norm, even_w_in, even_a_out_norm, even_b_conv, even_w_out, odd_mix_norm, odd_w_in, odd_c_q_norm, odd_c_kv_norm, odd_c_w_uq, odd_c_w_uk, odd_c_w_uv, odd_c_idx_wq, odd_c_idx_k_g, odd_c_idx_k_b, odd_d_v_g, odd_d_v_b, odd_d_w_s, odd_d_b_s, odd_w_out, xa_norm, xa_mem_norm, xa_wq, xa_wkv, xa_wo, ffn_norm, ffn_w13, ffn_w2, final_norm):
    s = x.shape[1]
    tm = _row_tile(s, 512)
    k_mem, v_mem = _mem_kv(mem, xa_mem_norm, xa_wkv)

    def xattn_ffn(x, layer, last, **mix):
        return _xattn_ffn(x, k_mem[layer], v_mem[layer], xa_norm[layer], xa_wq[layer], xa_wo[layer],
                          ffn_norm[layer], ffn_w13[layer], ffn_w2[layer], final_norm,
                          final_norm=last, tm=tm, **mix)

    x = _even_mixer(x, hgrn_lower_bounds, even_mix_norm[0], even_w_in[0], even_a_out_norm[0],
                    even_b_conv[0], even_w_out[0], tm=tm)
    x = xattn_ffn(x, 0, False)
    q_lat_t, q_idx_t, w_idx_t, ckv, ckv_t, k_idx, o_d = _odd_proj(
        x, odd_mix_norm[0], odd_w_in[0], odd_c_q_norm[0], odd_c_kv_norm[0], odd_c_w_uq[0], odd_c_w_uk[0],
        odd_c_idx_wq[0], odd_c_idx_k_g[0], odd_c_idx_k_b[0], odd_d_v_g[0], odd_d_v_b[0],
        odd_d_w_s[0], odd_d_b_s[0], tm=tm)
    o_c = _dsa(q_lat_t, q_idx_t, w_idx_t, k_idx, ckv, ckv_t, odd_c_w_uv[0])
    return xattn_ffn(x, 1, True, mix=(o_c, o_d), w_mix=odd_w_out[0])
```

```python
import functools

import jax
import jax.numpy as jnp
import numpy as np
from jax import lax
from jax.experimental import pallas as pl
from jax.experimental.pallas import tpu as pltpu

F32 = jnp.float32
BF16 = jnp.bfloat16
I32 = jnp.int32

V7X_VMEM_BYTES = 64 * 1024 * 1024
VMEM_LIMIT = 56 * 1024 * 1024
LANES = 128

A_HEADS, A_KDIM, A_VDIM, A_CHUNK = 4, 128, 128, 64
A_WIDTH = A_HEADS * A_VDIM
B_WIDTH = 512
C_HEADS, C_HEAD_DIM, C_Q_RANK, C_KV_RANK = 8, 64, 256, 128
C_IDX_HEADS, C_IDX_DIM, C_MAX_TOPK, C_QBLOCK = 8, 64, 256, 128
D_GROUPS, D_CHUNK, D_WIDTH = 4, 128, 512
X_HEADS = 4
HGRN_LEVELS = 6
DSA_KEY_CHUNK = 512
DSA_BITS_PER_CHECK = 4
DSA_SUM_ROWS = 16
DSA_GROUP_HEADS = 2
LOG2E = 1.4426950408889634
INT_MIN = -(2 ** 31)


def _dot(a, b):
    return jnp.dot(a, b, preferred_element_type=F32)


def _dot_nt(a, b):
    return lax.dot_general(a, b, (((1,), (1,)), ((), ())), preferred_element_type=F32)


def _dot_tn(a, b):
    return lax.dot_general(a, b, (((0,), (0,)), ((), ())), preferred_element_type=F32)


def _rms(x, g, eps=1e-6):
    return x * lax.rsqrt(jnp.mean(x * x, axis=-1, keepdims=True) + eps) * g


def _layer_norm(x, g, b, eps=1e-5):
    mu = jnp.mean(x, axis=-1, keepdims=True)
    xc = x - mu
    var = jnp.mean(xc * xc, axis=-1, keepdims=True)
    return xc * lax.rsqrt(var + eps) * g + b


def _silu(x):
    return x * jax.nn.sigmoid(x)


def _gelu(x):
    return 0.5 * x * (1.0 + lax.erf(x * (2.0 ** -0.5)))


def _const_spec(shape):
    zeros = (0,) * len(shape)
    return pl.BlockSpec(shape, lambda *_: zeros, pipeline_mode=pl.Buffered(1))


def _params(n_parallel=0, n_arbitrary=2):
    return pltpu.CompilerParams(
        dimension_semantics=("parallel",) * n_parallel + ("arbitrary",) * n_arbitrary,
        vmem_limit_bytes=VMEM_LIMIT)


def _mem_kv_kernel(mem_ref, g_ref, wkv_ref, k_ref, v_ref):
    d = mem_ref.shape[-1]
    m = _rms(mem_ref[0], g_ref[0]).astype(BF16)
    kv = _dot(m, wkv_ref[0])
    k_ref[0, 0] = kv[:, :d].astype(BF16)
    v_ref[0, 0] = kv[:, d:].astype(BF16)


def _mem_kv(mem, mem_g, wkv):
    depth, d = mem_g.shape
    bn, m_len, _ = mem.shape
    out = jax.ShapeDtypeStruct((depth, bn, m_len, d), BF16)
    return pl.pallas_call(
        _mem_kv_kernel,
        grid=(depth, bn),
        in_specs=[pl.BlockSpec((1, m_len, d), lambda l, b: (b, 0, 0)),
                  pl.BlockSpec((1, 1, d), lambda l, b: (l, 0, 0)),
                  pl.BlockSpec((1, d, 2 * d), lambda l, b: (l, 0, 0))],
        out_specs=[pl.BlockSpec((1, 1, m_len, d), lambda l, b: (l, b, 0, 0))] * 2,
        out_shape=[out, out],
        compiler_params=_params(),
        name="mem_kv",
    )(mem, mem_g.reshape(depth, 1, d), wkv.astype(BF16))


def _xattn_ffn_kernel(*refs, final_norm, ff_chunk, mix_parts):
    n_mix = mix_parts + 1 if mix_parts else 0
    mix_refs, refs = refs[:n_mix], refs[n_mix:]
    x_ref, k_ref, v_ref, gxa_ref, wq_ref, wo_ref, gff_ref, w13_ref, w2_ref, gfin_ref, o_ref = refs
    x = x_ref[0]
    if mix_parts:
        mixed = jnp.concatenate([r[0] for r in mix_refs[:mix_parts]], axis=-1)
        x = x + _dot(mixed, mix_refs[mix_parts][...])
    d = x.shape[-1]
    hd = d // X_HEADS
    ff = w2_ref.shape[0]
    h = _rms(x, gxa_ref[...]).astype(BF16)
    q = _dot(h, wq_ref[...])
    heads = []
    for i in range(X_HEADS):
        cols = slice(i * hd, (i + 1) * hd)
        sc = _dot_nt(q[:, cols].astype(BF16), k_ref[0, :, cols]) * (hd ** -0.5)
        p = jnp.exp(sc - jnp.max(sc, axis=-1, keepdims=True))
        o = _dot(p.astype(BF16), v_ref[0, :, cols])
        heads.append((o / jnp.sum(p, axis=-1, keepdims=True)).astype(BF16))
    x = x + _dot(jnp.concatenate(heads, axis=-1), wo_ref[...])
    h2 = _rms(x, gff_ref[...]).astype(BF16)
    acc = jnp.zeros_like(x)
    for c0 in range(0, ff, ff_chunk):
        g = _dot(h2, w13_ref[:, c0:c0 + ff_chunk])
        u = _dot(h2, w13_ref[:, ff + c0:ff + c0 + ff_chunk])
        acc = acc + _dot((_silu(g) * u).astype(BF16), w2_ref[c0:c0 + ff_chunk, :])
    x = x + acc
    if final_norm:
        x = _rms(x, gfin_ref[...])
    o_ref[0] = x


def _xattn_ffn(x, k_mem, v_mem, g_xa, wq, wo, g_ff, w13, w2, g_fin, *, final_norm, tm, mix=(), w_mix=None):
    bn, s, d = x.shape
    mix_specs = [pl.BlockSpec((1, tm, a.shape[-1]), lambda b, i: (b, i, 0)) for a in mix]
    mix_args = list(mix)
    if mix:
        mix_specs.append(_const_spec(w_mix.shape))
        mix_args.append(w_mix.astype(BF16))
    m_len = k_mem.shape[1]
    ff = w2.shape[0]
    ff_chunk = ff // 2 if (ff // 2) % LANES == 0 else ff
    row = lambda a: a.reshape(1, d)
    kern = functools.partial(_xattn_ffn_kernel, final_norm=final_norm, ff_chunk=ff_chunk, mix_parts=len(mix))
    return pl.pallas_call(
        kern,
        grid=(bn, s // tm),
        in_specs=mix_specs + [
                  pl.BlockSpec((1, tm, d), lambda b, i: (b, i, 0)),
                  pl.BlockSpec((1, m_len, d), lambda b, i: (b, 0, 0)),
                  pl.BlockSpec((1, m_len, d), lambda b, i: (b, 0, 0)),
                  _const_spec((1, d)), _const_spec((d, d)), _const_spec((d, d)),
                  _const_spec((1, d)), _const_spec((d, 2 * ff)), _const_spec((ff, d)),
                  _const_spec((1, d))],
        out_specs=pl.BlockSpec((1, tm, d), lambda b, i: (b, i, 0)),
        out_shape=jax.ShapeDtypeStruct(x.shape, F32),
        compiler_params=_params(),
        name="xattn_ffn_final" if final_norm else "xattn_ffn",
    )(*mix_args, x, k_mem, v_mem, row(g_xa), wq.astype(BF16), wo.astype(BF16), row(g_ff),
      w13.astype(BF16), w2.astype(BF16), row(g_fin))


def _hgrn_tables():
    n = A_CHUNK
    t = np.arange(n)[:, None]
    u = np.arange(n)[None, :]
    sums, pair = [], []
    for lvl in range(HGRN_LEVELS):
        c = n >> (lvl + 1)
        mid = (t // (2 * c)) * (2 * c) + c
        late = t >= mid
        sums.append(np.where(late, (u >= mid) & (u <= t), (u > t) & (u < mid)))
        pair.append(late & (u < mid) & (u >= mid - c))
    sums.append(u <= t)
    sums.append(u > t)
    pair.append(t == u)
    return (np.concatenate(sums, axis=0).astype(np.float32),
            np.stack(pair, axis=0).astype(np.float32))


def _even_kernel(x_ref, lbp_ref, g_ref, win_ref, aon_ref, conv_ref, wout_ref, sums_ref, pair_ref,
                 o_ref, q_scr, k_scr, v_scr, lfh_scr, lfl_scr, oa_scr, z_scr, st_scr):
    tm = x_ref.shape[1]
    n = A_CHUNK

    @pl.when(pl.program_id(1) == 0)
    def _():
        st_scr[...] = jnp.zeros_like(st_scr)
        z_scr[0:8, :] = jnp.zeros((8, B_WIDTH), F32)

    x = x_ref[0]
    h = _rms(x, g_ref[...]).astype(BF16)

    def proj(j, width=A_WIDTH):
        return _dot(h, win_ref[:, j * width:(j + 1) * width])

    lbp = lbp_ref[...]
    e = jnp.exp(lbp - jnp.max(lbp, axis=0, keepdims=True))
    lb = e[0:1, :] / jnp.sum(e, axis=0, keepdims=True)

    q_scr[...] = _silu(proj(0))
    f = lb + (1.0 - lb) * jax.nn.sigmoid(proj(1))
    k_scr[...] = 1.0 - f
    logf = jnp.log(f)
    lf_hi = logf.astype(BF16)
    lfh_scr[...] = lf_hi
    lfl_scr[...] = (logf - lf_hi.astype(F32)).astype(BF16)
    v_scr[...] = proj(2)

    row = lax.broadcasted_iota(I32, (n, 1), 0)

    def chunk(c, carry):
        r0 = pl.multiple_of(c * n, n)
        rows = pl.ds(r0, n)
        sums = sums_ref[...]
        ex = jnp.exp(_dot(sums, lfh_scr[rows, :]) + _dot(sums, lfl_scr[rows, :]))
        qc, kc, vc = q_scr[rows, :], k_scr[rows, :], v_scr[rows, :]
        for hh in range(A_HEADS):
            cols = slice(hh * A_KDIM, (hh + 1) * A_KDIM)
            qh, kh = qc[:, cols], kc[:, cols]
            attn = _dot_nt(qh.astype(BF16), kh.astype(BF16)) * pair_ref[HGRN_LEVELS]
            for lvl in range(HGRN_LEVELS):
                late = (row & (n >> (lvl + 1))) != 0
                xl = (jnp.where(late, qh, kh) * ex[lvl * n:(lvl + 1) * n, cols]).astype(BF16)
                attn = attn + _dot_nt(xl, xl) * pair_ref[lvl]
            e_in = ex[HGRN_LEVELS * n:(HGRN_LEVELS + 1) * n, cols]
            e_out = ex[(HGRN_LEVELS + 1) * n:(HGRN_LEVELS + 2) * n, cols]
            st = st_scr[hh]
            vh = vc[:, cols].astype(BF16)
            o = _dot(attn.astype(BF16), vh) + _dot_nt((qh * e_in).astype(BF16), st.astype(BF16))
            oa_scr[rows, cols] = o
            st_scr[hh] = st * e_in[n - 1:n, :] + _dot_tn(vh, (kh * e_out).astype(BF16))
        return carry

    lax.fori_loop(0, tm // n, chunk, 0, unroll=2)

    gate = _silu(proj(3))
    parts = []
    for hh in range(A_HEADS):
        cols = slice(hh * A_VDIM, (hh + 1) * A_VDIM)
        parts.append((_rms(oa_scr[:, cols], aon_ref[hh:hh + 1, :]) * gate[:, cols]).astype(BF16))

    z_scr[8:8 + tm, :] = proj(5) * proj(6)
    w = conv_ref[...]
    y = (w[0:1, :] * z_scr[6:6 + tm, :] + w[1:2, :] * z_scr[7:7 + tm, :] + w[2:3, :] * z_scr[8:8 + tm, :])
    z_scr[0:8, :] = z_scr[tm:tm + 8, :]
    parts.append((proj(4) * y).astype(BF16))

    o_ref[0] = x + _dot(jnp.concatenate(parts, axis=-1), wout_ref[...])


def _even_mixer(x, lbp, g, w_in, a_out_norm, b_conv, w_out, *, tm):
    bn, s, d = x.shape
    sums, pair = _hgrn_tables()
    n_in = w_in.shape[1]
    return pl.pallas_call(
        _even_kernel,
        grid=(bn, s // tm),
        in_specs=[pl.BlockSpec((1, tm, d), lambda b, i: (b, i, 0)),
                  _const_spec(lbp.shape), _const_spec((1, d)), _const_spec((d, n_in)),
                  _const_spec(a_out_norm.shape), _const_spec(b_conv.shape),
                  _const_spec(w_out.shape), _const_spec(sums.shape), _const_spec(pair.shape)],
        out_specs=pl.BlockSpec((1, tm, d), lambda b, i: (b, i, 0)),
        out_shape=jax.ShapeDtypeStruct(x.shape, F32),
        scratch_shapes=[pltpu.VMEM((tm, A_WIDTH), F32), pltpu.VMEM((tm, A_WIDTH), F32),
                        pltpu.VMEM((tm, A_WIDTH), F32), pltpu.VMEM((tm, A_WIDTH), BF16),
                        pltpu.VMEM((tm, A_WIDTH), BF16), pltpu.VMEM((tm, A_WIDTH), F32),
                        pltpu.VMEM((tm + 8, B_WIDTH), F32),
                        pltpu.VMEM((A_HEADS, A_VDIM, A_KDIM), F32)],
        compiler_params=_params(),
        name="even_mixer",
    )(x, lbp, g.reshape(1, d), w_in.astype(BF16), a_out_norm, b_conv, w_out.astype(BF16),
      jnp.asarray(sums, BF16), jnp.asarray(pair, F32))


def _odd_proj_kernel(x_ref, g_ref, win_ref, wwt_ref, qg_ref, kvg_ref, wuq_ref, wukt_ref, iwqt_ref, ikg_ref, ikb_ref,
                     vg_ref, vb_ref, ws_ref, bs_ref,
                     qlat_t_ref, qidx_t_ref, widx_t_ref, ckv_ref, ckv_t_ref, kidx_ref, od_ref):
    tm = x_ref.shape[1]
    tq = C_QBLOCK
    kc = ckv_t_ref.shape[-1]
    h = _rms(x_ref[0], g_ref[...]).astype(BF16)
    o = 0

    def proj(width):
        nonlocal o
        r = _dot(h, win_ref[:, o:o + width])
        o += width
        return r

    cq = _rms(proj(C_Q_RANK), qg_ref[...]).astype(BF16)
    ckv = _rms(proj(C_KV_RANK), kvg_ref[...])
    ckv_ref[0] = ckv.astype(BF16)
    for j in range(tm // kc):
        ckv_t_ref[0, j, :C_KV_RANK, :] = ckv[j * kc:(j + 1) * kc, :].T.astype(BF16)
        ckv_t_ref[0, j, C_KV_RANK:, :] = jnp.ones((DSA_SUM_ROWS, kc), BF16)
    du = proj(D_WIDTH)
    dv = proj(D_WIDTH)
    kidx_ref[0] = _layer_norm(proj(LANES)[:, :C_IDX_DIM], ikg_ref[...], ikb_ref[...]).astype(BF16)
    widx_t_ref[0] = _dot_nt(wwt_ref[...], h)[:C_IDX_HEADS, :] * (C_IDX_HEADS ** -0.5)

    q = _dot(cq, wuq_ref[...])
    for i in range(C_HEADS):
        qh = q[:, i * C_HEAD_DIM:(i + 1) * C_HEAD_DIM].astype(BF16)
        lat_t = (_dot_nt(wukt_ref[i], qh) * (C_HEAD_DIM ** -0.5 * LOG2E)).astype(BF16)
        idx_t = (_dot_nt(iwqt_ref[i], cq) * (C_IDX_DIM ** -0.5)).astype(BF16)
        g, k = divmod(i, DSA_GROUP_HEADS)
        for j in range(tm // tq):
            qlat_t_ref[0, j, g, :, k * tq:(k + 1) * tq] = lat_t[:, j * tq:(j + 1) * tq]
            qidx_t_ref[0, j, :, i * tq:(i + 1) * tq] = idx_t[:, j * tq:(j + 1) * tq]

    u = _gelu(du)
    v = _layer_norm(_gelu(dv), vg_ref[...], vb_ref[...]).astype(BF16)
    tri = (lax.broadcasted_iota(I32, (D_CHUNK, D_CHUNK), 0) >= lax.broadcasted_iota(I32, (D_CHUNK, D_CHUNK), 1))
    gw = D_WIDTH // D_GROUPS
    for gi in range(D_GROUPS):
        wc = jnp.where(tri, ws_ref[gi], 0.0).astype(BF16)
        cols = slice(gi * gw, (gi + 1) * gw)
        for c0 in range(0, tm, D_CHUNK):
            mixed = _dot(wc, v[c0:c0 + D_CHUNK, cols]) + bs_ref[:, gi:gi + 1]
            od_ref[0, c0:c0 + D_CHUNK, cols] = (u[c0:c0 + D_CHUNK, cols] * mixed).astype(BF16)


def _odd_proj(x, g, w_in, q_g, kv_g, w_uq, w_uk, idx_wq, ik_g, ik_b, v_g, v_b, w_s, b_s, *, tm):
    bn, s, d = x.shape
    c0, c1, c2, c3 = C_Q_RANK, C_Q_RANK + C_KV_RANK, C_Q_RANK + C_KV_RANK + C_IDX_DIM, \
        C_Q_RANK + C_KV_RANK + C_IDX_DIM + C_IDX_HEADS
    pad = LANES - C_IDX_DIM
    w = jnp.concatenate([w_in[:, :c1], w_in[:, c3:], w_in[:, c1:c2], jnp.zeros((d, pad), w_in.dtype)],
                        axis=1).astype(BF16)
    ww_t = jnp.concatenate([w_in[:, c2:c3].T, jnp.zeros((16 - C_IDX_HEADS, d), w_in.dtype)], axis=0).astype(BF16)
    wuk_t = jnp.transpose(w_uk, (0, 2, 1)).astype(BF16)
    iwq_t = jnp.transpose(idx_wq.reshape(C_Q_RANK, C_IDX_HEADS, C_IDX_DIM), (1, 2, 0)).astype(BF16)
    kc = min(DSA_KEY_CHUNK, s)
    assert tm % kc == 0 and tm % C_QBLOCK == 0
    row = lambda a: a.reshape(1, -1)
    shp = lambda n, dt: jax.ShapeDtypeStruct((bn, s, n), dt)
    blk = lambda n: pl.BlockSpec((1, tm, n), lambda b, i: (b, i, 0))
    nq = s // C_QBLOCK
    heads_q = C_HEADS * C_QBLOCK
    n_groups, group_q = C_HEADS // DSA_GROUP_HEADS, DSA_GROUP_HEADS * C_QBLOCK
    qblk = lambda n: pl.BlockSpec((1, tm // C_QBLOCK, n, heads_q), lambda b, i: (b, i, 0, 0))
    return pl.pallas_call(
        _odd_proj_kernel,
        grid=(bn, s // tm),
        in_specs=[blk(d), _const_spec((1, d)), _const_spec(w.shape), _const_spec(ww_t.shape),
                  _const_spec((1, C_Q_RANK)), _const_spec((1, C_KV_RANK)),
                  _const_spec(w_uq.shape), _const_spec(wuk_t.shape), _const_spec(iwq_t.shape),
                  _const_spec((1, C_IDX_DIM)), _const_spec((1, C_IDX_DIM)),
                  _const_spec((1, D_WIDTH)), _const_spec((1, D_WIDTH)),
                  _const_spec(w_s.shape), _const_spec((D_CHUNK, D_GROUPS))],
        out_specs=[pl.BlockSpec((1, tm // C_QBLOCK, n_groups, C_KV_RANK, group_q), lambda b, i: (b, i, 0, 0, 0)),
                   qblk(C_IDX_DIM),
                   pl.BlockSpec((1, C_IDX_HEADS, tm), lambda b, i: (b, 0, i)),
                   blk(C_KV_RANK),
                   pl.BlockSpec((1, tm // kc, C_KV_RANK + DSA_SUM_ROWS, kc), lambda b, i: (b, i, 0, 0)),
                   blk(C_IDX_DIM), blk(D_WIDTH)],
        out_shape=[jax.ShapeDtypeStruct((bn, nq, n_groups, C_KV_RANK, group_q), BF16),
                   jax.ShapeDtypeStruct((bn, nq, C_IDX_DIM, heads_q), BF16),
                   jax.ShapeDtypeStruct((bn, C_IDX_HEADS, s), F32),
                   shp(C_KV_RANK, BF16),
                   jax.ShapeDtypeStruct((bn, s // kc, C_KV_RANK + DSA_SUM_ROWS, kc), BF16),
                   shp(C_IDX_DIM, BF16), shp(D_WIDTH, BF16)],
        compiler_params=_params(),
        name="odd_proj",
    )(x, row(g), w, ww_t, row(q_g), row(kv_g), w_uq.astype(BF16), wuk_t, iwq_t,
      row(ik_g), row(ik_b), row(v_g), row(v_b), w_s, b_s.T)


def _order_key(score):
    bits = lax.bitcast_convert_type(score + 0.0, I32)
    return bits ^ ((bits >> 31) & 0x7FFFFFFF)


def _dsa_kernel(qlat_t_ref, qidx_t_ref, widx_t_ref, kidx_ref, ckv_ref, ckv_t_ref, wuv_ref,
                o_ref, key_scr, s_scr, mx_scr, m_scr, acc_scr, *, topk):
    tq = C_QBLOCK
    kc = key_scr.shape[1]
    i = pl.program_id(1)
    n_chunks = ((i + 1) * tq + kc - 1) // kc
    t_pos = i * tq + lax.broadcasted_iota(I32, (1, tq), 1)

    qidx_t = qidx_t_ref[0, 0]
    widx_t = widx_t_ref[0]

    def score_chunk(c, carry):
        kb = kidx_ref[0, pl.ds(pl.multiple_of(c * kc, kc), kc), :]
        logits = _dot(kb, qidx_t)
        score = jnp.zeros((kc, tq), F32)
        for hh in range(C_IDX_HEADS):
            score = score + jnp.maximum(logits[:, hh * tq:(hh + 1) * tq], 0.0) * widx_t[hh:hh + 1, :]
        key_pos = c * kc + lax.broadcasted_iota(I32, (kc, 1), 0)
        key_scr[c] = jnp.where(key_pos <= t_pos, _order_key(score), INT_MIN)
        return carry

    lax.fori_loop(0, n_chunks, score_chunk, 0)

    def count_ge(thr):
        fold = 4 * 8
        def body(c, acc):
            hit = jnp.where(key_scr[c] >= thr, 1.0, 0.0)
            return acc + jnp.sum(hit.reshape(kc // fold, fold, tq), axis=0)
        acc = lax.fori_loop(0, n_chunks, body, jnp.zeros((fold, tq), F32))
        return jnp.sum(acc, axis=0, keepdims=True)

    def unresolved(state):
        it, _, cnt = state
        return jnp.logical_and(it < 32, jnp.max(cnt) > topk)

    def bit_steps(state):
        it, prefix, cnt = state
        for _ in range(DSA_BITS_PER_CHECK):
            trial = prefix | (jnp.int32(1) << (31 - it))
            c = count_ge(trial ^ INT_MIN)
            keep = c >= topk
            it, prefix, cnt = it + 1, jnp.where(keep, trial, prefix), jnp.where(keep, c, cnt)
        return it, prefix, cnt

    n_causal = (t_pos + 1).astype(F32)
    _, prefix, n_ge = lax.while_loop(unresolved, bit_steps, (jnp.int32(0), jnp.zeros((1, tq), I32), n_causal))
    thr = jnp.maximum(prefix ^ INT_MIN, INT_MIN + 1)

    has_surplus = n_ge > topk

    @pl.when(jnp.max(n_ge) > topk)
    def _():
        n_gt = count_ge(thr + 1)
        need = jnp.where(has_surplus, topk - n_gt, float(kc) * 1e6)
        lower = (lax.broadcasted_iota(I32, (kc, kc), 0) >= lax.broadcasted_iota(I32, (kc, kc), 1))
        lower = jnp.where(lower, 1.0, 0.0).astype(BF16)

        def body(c, seen):
            keys = key_scr[c]
            eq = keys == thr
            rank = seen + _dot(lower, jnp.where(eq, 1.0, 0.0).astype(BF16))
            key_scr[c] = jnp.where(jnp.logical_and(eq, rank > need), INT_MIN, keys)
            return rank[kc - 1:kc, :]

        lax.fori_loop(0, n_chunks, body, jnp.zeros((1, tq), F32))

    n_groups = s_scr.shape[1]
    m_scr[...] = jnp.full(m_scr.shape, -jnp.inf, F32)
    acc_scr[...] = jnp.zeros(acc_scr.shape, F32)

    def pipeline_step(c_next, c_now):
        if c_next is not None:
            c_next, slot_next = c_next
            kv = ckv_ref[0, pl.ds(pl.multiple_of(c_next * kc, kc), kc), :]
            bias = jnp.where(key_scr[c_next] >= thr, 0.0, -jnp.inf)
            bias = jnp.concatenate([bias] * DSA_GROUP_HEADS, axis=1)
        if c_now is not None:
            c_now, slot_now = c_now
            kv_t = ckv_t_ref[0, c_now]
        for g in range(n_groups):
            if c_next is not None:
                sb = _dot(kv, qlat_t_ref[0, 0, g]) + bias
                s_scr[slot_next, g] = sb
                mx_scr[slot_next, g] = jnp.max(sb, axis=0, keepdims=True)
            if c_now is not None:
                m_old = m_scr[g]
                m_new = jnp.maximum(m_old, mx_scr[slot_now, g])
                m_safe = jnp.where(m_new == -jnp.inf, 0.0, m_new)
                alpha = jnp.exp2(m_old - m_safe)
                p = jnp.exp2(s_scr[slot_now, g] - m_safe)
                m_scr[g] = m_new
                acc_scr[g] = alpha * acc_scr[g] + _dot(kv_t, p.astype(BF16))

    pipeline_step((jnp.int32(0), 0), None)
    n_pairs = (n_chunks - 1) // 2

    def attn_pair(j, carry):
        c = 2 * j
        pipeline_step((c + 1, 1), (c, 0))
        pipeline_step((c + 2, 0), (c + 1, 1))
        return carry

    lax.fori_loop(0, n_pairs, attn_pair, 0)
    last = n_chunks - 1

    @pl.when(last == 2 * n_pairs)
    def _():
        pipeline_step(None, (last, 0))

    @pl.when(last != 2 * n_pairs)
    def _():
        pipeline_step((last, 1), (last - 1, 0))
        pipeline_step(None, (last, 1))
    outs = []
    for hh in range(C_HEADS):
        g, k = divmod(hh, DSA_GROUP_HEADS)
        cols = slice(k * tq, (k + 1) * tq)
        o_lat_t = acc_scr[g, :C_KV_RANK, cols] / acc_scr[g, C_KV_RANK:C_KV_RANK + 1, cols]
        outs.append(_dot_tn(o_lat_t.astype(BF16), wuv_ref[hh]).astype(BF16))
    o_ref[0] = jnp.concatenate(outs, axis=-1)


def _dsa(q_lat_t, q_idx_t, w_idx_t, k_idx, ckv, ckv_t, w_uv):
    bn, s, _ = ckv.shape
    tq = C_QBLOCK
    kc = ckv_t.shape[-1]
    topk = min(C_MAX_TOPK, s // 4)
    heads_q = C_HEADS * tq
    n_groups, group_q = q_lat_t.shape[2], q_lat_t.shape[4]
    full = lambda n: pl.BlockSpec((1, s, n), lambda b, i: (b, 0, 0))
    width = C_HEADS * C_HEAD_DIM
    return pl.pallas_call(
        functools.partial(_dsa_kernel, topk=float(topk)),
        grid=(bn, s // tq),
        in_specs=[pl.BlockSpec((1, 1, n_groups, C_KV_RANK, group_q), lambda b, i: (b, i, 0, 0, 0)),
                  pl.BlockSpec((1, 1, C_IDX_DIM, heads_q), lambda b, i: (b, i, 0, 0)),
                  pl.BlockSpec((1, C_IDX_HEADS, tq), lambda b, i: (b, 0, i)),
                  full(C_IDX_DIM), full(C_KV_RANK),
                  pl.BlockSpec((1, s // kc, ckv_t.shape[2], kc), lambda b, i: (b, 0, 0, 0)),
                  _const_spec(w_uv.shape)],
        out_specs=pl.BlockSpec((1, tq, width), lambda b, i: (b, i, 0)),
        out_shape=jax.ShapeDtypeStruct((bn, s, width), BF16),
        scratch_shapes=[pltpu.VMEM((s // kc, kc, tq), I32), pltpu.VMEM((2, n_groups, kc, group_q), F32),
                        pltpu.VMEM((2, n_groups, 1, group_q), F32),
                        pltpu.VMEM((n_groups, 1, group_q), F32),
                        pltpu.VMEM((n_groups, ckv_t.shape[2], group_q), F32)],
        compiler_params=_params(),
        name="dsa",
    )(q_lat_t, q_idx_t, w_idx_t, k_idx, ckv, ckv_t, w_uv.astype(BF16))


def _row_tile(s, want):
    return want if s % want == 0 else s


def kernel(x, mem, hgrn_lower_bounds, even_mix_norm, even_w_in, even_a_out_norm, even_b_conv, even_w_out, odd_mix_norm, odd_w_in, odd_c_q_norm, odd_c_kv_norm, odd_c_w_uq, odd_c_w_uk, odd_c_w_uv, odd_c_idx_wq, odd_c_idx_k_g, odd_c_idx_k_b, odd_d_v_g, odd_d_v_b, odd_d_w_s, odd_d_b_s, odd_w_out, xa_norm, xa_mem_norm, xa_wq, xa_wkv, xa_wo, ffn_norm, ffn_w13, ffn_w2, final_norm):
    s = x.shape[1]
    tm = _row_tile(s, 512)
    k_mem, v_mem = _mem_kv(mem, xa_mem_norm, xa_wkv)

    def xattn_ffn(x, layer, last, **mix):
        return _xattn_ffn(x, k_mem[layer], v_mem[layer], xa_norm[layer], xa_wq[layer], xa_wo[layer],
                          ffn_norm[layer], ffn_w13[layer], ffn_w2[layer], final_norm,
                          final_norm=last, tm=tm, **mix)

    x = _even_mixer(x, hgrn_lower_bounds, even_mix_norm[0], even_w_in[0], even_a_out_norm[0],
                    even_b_conv[0], even_w_out[0], tm=tm)
    x = xattn_ffn(x, 0, False)
    q_lat_t, q_idx_t, w_idx_t, ckv, ckv_t, k_idx, o_d = _odd_proj(
        x, odd_mix_norm[0], odd_w_in[0], odd_c_q_norm[0], odd_c_kv_norm[0], odd_c_w_uq[0], odd_c_w_uk[0],
        odd_c_idx_wq[0], odd_c_idx_k_g[0], odd_c_idx_k_b[0], odd_d_v_g[0], odd_d_v_b[0],
        odd_d_w_s[0], odd_d_b_s[0], tm=tm)
    o_c = _dsa(q_lat_t, q_idx_t, w_idx_t, k_idx, ckv, ckv_t, odd_c_w_uv[0])
    return xattn_ffn(x, 1, True, mix=(o_c, o_d), w_mix=odd_w_out[0])
```

```python
import functools

import jax
import jax.numpy as jnp
import numpy as np
from jax import lax
from jax.experimental import pallas as pl
from jax.experimental.pallas import tpu as pltpu

F32 = jnp.float32
BF16 = jnp.bfloat16
I32 = jnp.int32

V7X_VMEM_BYTES = 64 * 1024 * 1024
VMEM_LIMIT = 56 * 1024 * 1024
LANES = 128

A_HEADS, A_KDIM, A_VDIM, A_CHUNK = 4, 128, 128, 64
A_WIDTH = A_HEADS * A_VDIM
B_WIDTH = 512
C_HEADS, C_HEAD_DIM, C_Q_RANK, C_KV_RANK = 8, 64, 256, 128
C_IDX_HEADS, C_IDX_DIM, C_MAX_TOPK = 8, 64, 256
D_GROUPS, D_CHUNK, D_WIDTH = 4, 128, 512
X_HEADS = 4
HGRN_LEVELS = 6
DSA_KEY_CHUNK = 512
DSA_BITS_PER_CHECK = 4
DSA_SUM_ROWS = 16
DSA_QUERIES = 256
DSA_GROUP_HEADS = 1
LOG2E = 1.4426950408889634
INT_MIN = -(2 ** 31)


def _dot(a, b):
    return jnp.dot(a, b, preferred_element_type=F32)


def _dot_nt(a, b):
    return lax.dot_general(a, b, (((1,), (1,)), ((), ())), preferred_element_type=F32)


def _dot_tn(a, b):
    return lax.dot_general(a, b, (((0,), (0,)), ((), ())), preferred_element_type=F32)


def _rms(x, g, eps=1e-6):
    return x * lax.rsqrt(jnp.mean(x * x, axis=-1, keepdims=True) + eps) * g


def _layer_norm(x, g, b, eps=1e-5):
    mu = jnp.mean(x, axis=-1, keepdims=True)
    xc = x - mu
    var = jnp.mean(xc * xc, axis=-1, keepdims=True)
    return xc * lax.rsqrt(var + eps) * g + b


def _silu(x):
    return x * jax.nn.sigmoid(x)


def _gelu(x):
    return 0.5 * x * (1.0 + lax.erf(x * (2.0 ** -0.5)))


def _const_spec(shape):
    zeros = (0,) * len(shape)
    return pl.BlockSpec(shape, lambda *_: zeros, pipeline_mode=pl.Buffered(1))


def _params(n_parallel=0, n_arbitrary=2):
    return pltpu.CompilerParams(
        dimension_semantics=("parallel",) * n_parallel + ("arbitrary",) * n_arbitrary,
        vmem_limit_bytes=VMEM_LIMIT)


def _mem_kv_kernel(mem_ref, g_ref, wkv_ref, k_ref, v_ref):
    d = mem_ref.shape[-1]
    m = _rms(mem_ref[0], g_ref[0]).astype(BF16)
    kv = _dot(m, wkv_ref[0])
    k_ref[0, 0] = kv[:, :d].astype(BF16)
    v_ref[0, 0] = kv[:, d:].astype(BF16)


def _mem_kv(mem, mem_g, wkv):
    depth, d = mem_g.shape
    bn, m_len, _ = mem.shape
    out = jax.ShapeDtypeStruct((depth, bn, m_len, d), BF16)
    return pl.pallas_call(
        _mem_kv_kernel,
        grid=(depth, bn),
        in_specs=[pl.BlockSpec((1, m_len, d), lambda l, b: (b, 0, 0)),
                  pl.BlockSpec((1, 1, d), lambda l, b: (l, 0, 0)),
                  pl.BlockSpec((1, d, 2 * d), lambda l, b: (l, 0, 0))],
        out_specs=[pl.BlockSpec((1, 1, m_len, d), lambda l, b: (l, b, 0, 0))] * 2,
        out_shape=[out, out],
        compiler_params=_params(),
        name="mem_kv",
    )(mem, mem_g.reshape(depth, 1, d), wkv.astype(BF16))


def _xattn_ffn_kernel(*refs, final_norm, ff_chunk, mix_parts):
    n_mix = mix_parts + 1 if mix_parts else 0
    mix_refs, refs = refs[:n_mix], refs[n_mix:]
    x_ref, k_ref, v_ref, gxa_ref, wq_ref, wo_ref, gff_ref, w13_ref, w2_ref, gfin_ref, o_ref = refs
    x = x_ref[0]
    if mix_parts:
        mixed = jnp.concatenate([r[0] for r in mix_refs[:mix_parts]], axis=-1)
        x = x + _dot(mixed, mix_refs[mix_parts][...])
    d = x.shape[-1]
    hd = d // X_HEADS
    ff = w2_ref.shape[0]
    h = _rms(x, gxa_ref[...]).astype(BF16)
    q = _dot(h, wq_ref[...])
    heads = []
    for i in range(X_HEADS):
        cols = slice(i * hd, (i + 1) * hd)
        sc = _dot_nt(q[:, cols].astype(BF16), k_ref[0, :, cols]) * (hd ** -0.5)
        p = jnp.exp(sc - jnp.max(sc, axis=-1, keepdims=True))
        o = _dot(p.astype(BF16), v_ref[0, :, cols])
        heads.append((o / jnp.sum(p, axis=-1, keepdims=True)).astype(BF16))
    x = x + _dot(jnp.concatenate(heads, axis=-1), wo_ref[...])
    h2 = _rms(x, gff_ref[...]).astype(BF16)
    acc = jnp.zeros_like(x)
    for c0 in range(0, ff, ff_chunk):
        g = _dot(h2, w13_ref[:, c0:c0 + ff_chunk])
        u = _dot(h2, w13_ref[:, ff + c0:ff + c0 + ff_chunk])
        acc = acc + _dot((_silu(g) * u).astype(BF16), w2_ref[c0:c0 + ff_chunk, :])
    x = x + acc
    if final_norm:
        x = _rms(x, gfin_ref[...])
    o_ref[0] = x


def _xattn_ffn(x, k_mem, v_mem, g_xa, wq, wo, g_ff, w13, w2, g_fin, *, final_norm, tm, mix=(), w_mix=None):
    bn, s, d = x.shape
    mix_specs = [pl.BlockSpec((1, tm, a.shape[-1]), lambda b, i: (b, i, 0)) for a in mix]
    mix_args = list(mix)
    if mix:
        mix_specs.append(_const_spec(w_mix.shape))
        mix_args.append(w_mix.astype(BF16))
    m_len = k_mem.shape[1]
    ff = w2.shape[0]
    ff_chunk = ff // 2 if (ff // 2) % LANES == 0 else ff
    row = lambda a: a.reshape(1, d)
    kern = functools.partial(_xattn_ffn_kernel, final_norm=final_norm, ff_chunk=ff_chunk, mix_parts=len(mix))
    return pl.pallas_call(
        kern,
        grid=(bn, s // tm),
        in_specs=mix_specs + [
                  pl.BlockSpec((1, tm, d), lambda b, i: (b, i, 0)),
                  pl.BlockSpec((1, m_len, d), lambda b, i: (b, 0, 0)),
                  pl.BlockSpec((1, m_len, d), lambda b, i: (b, 0, 0)),
                  _const_spec((1, d)), _const_spec((d, d)), _const_spec((d, d)),
                  _const_spec((1, d)), _const_spec((d, 2 * ff)), _const_spec((ff, d)),
                  _const_spec((1, d))],
        out_specs=pl.BlockSpec((1, tm, d), lambda b, i: (b, i, 0)),
        out_shape=jax.ShapeDtypeStruct(x.shape, F32),
        compiler_params=_params(),
        name="xattn_ffn_final" if final_norm else "xattn_ffn",
    )(*mix_args, x, k_mem, v_mem, row(g_xa), wq.astype(BF16), wo.astype(BF16), row(g_ff),
      w13.astype(BF16), w2.astype(BF16), row(g_fin))


def _hgrn_tables():
    n = A_CHUNK
    t = np.arange(n)[:, None]
    u = np.arange(n)[None, :]
    sums, pair = [], []
    for lvl in range(HGRN_LEVELS):
        c = n >> (lvl + 1)
        mid = (t // (2 * c)) * (2 * c) + c
        late = t >= mid
        sums.append(np.where(late, (u >= mid) & (u <= t), (u > t) & (u < mid)))
        pair.append(late & (u < mid) & (u >= mid - c))
    sums.append(u <= t)
    sums.append(u > t)
    pair.append(t == u)
    return (np.concatenate(sums, axis=0).astype(np.float32),
            np.stack(pair, axis=0).astype(np.float32))


def _even_kernel(x_ref, lbp_ref, g_ref, win_ref, aon_ref, conv_ref, wout_ref, sums_ref, pair_ref,
                 o_ref, q_scr, k_scr, v_scr, lfh_scr, lfl_scr, oa_scr, z_scr, st_scr):
    tm = x_ref.shape[1]
    n = A_CHUNK

    @pl.when(pl.program_id(1) == 0)
    def _():
        st_scr[...] = jnp.zeros_like(st_scr)
        z_scr[0:8, :] = jnp.zeros((8, B_WIDTH), F32)

    x = x_ref[0]
    h = _rms(x, g_ref[...]).astype(BF16)

    def proj(j, width=A_WIDTH):
        return _dot(h, win_ref[:, j * width:(j + 1) * width])

    lbp = lbp_ref[...]
    e = jnp.exp(lbp - jnp.max(lbp, axis=0, keepdims=True))
    lb = e[0:1, :] / jnp.sum(e, axis=0, keepdims=True)

    q_scr[...] = _silu(proj(0))
    f = lb + (1.0 - lb) * jax.nn.sigmoid(proj(1))
    k_scr[...] = 1.0 - f
    logf = jnp.log(f)
    lf_hi = logf.astype(BF16)
    lfh_scr[...] = lf_hi
    lfl_scr[...] = (logf - lf_hi.astype(F32)).astype(BF16)
    v_scr[...] = proj(2)

    row = lax.broadcasted_iota(I32, (n, 1), 0)

    def chunk(c, carry):
        r0 = pl.multiple_of(c * n, n)
        rows = pl.ds(r0, n)
        sums = sums_ref[...]
        ex = jnp.exp(_dot(sums, lfh_scr[rows, :]) + _dot(sums, lfl_scr[rows, :]))
        qc, kc, vc = q_scr[rows, :], k_scr[rows, :], v_scr[rows, :]
        for hh in range(A_HEADS):
            cols = slice(hh * A_KDIM, (hh + 1) * A_KDIM)
            qh, kh = qc[:, cols], kc[:, cols]
            attn = _dot_nt(qh.astype(BF16), kh.astype(BF16)) * pair_ref[HGRN_LEVELS]
            for lvl in range(HGRN_LEVELS):
                late = (row & (n >> (lvl + 1))) != 0
                xl = (jnp.where(late, qh, kh) * ex[lvl * n:(lvl + 1) * n, cols]).astype(BF16)
                attn = attn + _dot_nt(xl, xl) * pair_ref[lvl]
            e_in = ex[HGRN_LEVELS * n:(HGRN_LEVELS + 1) * n, cols]
            e_out = ex[(HGRN_LEVELS + 1) * n:(HGRN_LEVELS + 2) * n, cols]
            st = st_scr[hh]
            vh = vc[:, cols].astype(BF16)
            o = _dot(attn.astype(BF16), vh) + _dot_nt((qh * e_in).astype(BF16), st.astype(BF16))
            oa_scr[rows, cols] = o
            st_scr[hh] = st * e_in[n - 1:n, :] + _dot_tn(vh, (kh * e_out).astype(BF16))
        return carry

    lax.fori_loop(0, tm // n, chunk, 0, unroll=2)

    gate = _silu(proj(3))
    parts = []
    for hh in range(A_HEADS):
        cols = slice(hh * A_VDIM, (hh + 1) * A_VDIM)
        parts.append((_rms(oa_scr[:, cols], aon_ref[hh:hh + 1, :]) * gate[:, cols]).astype(BF16))

    z_scr[8:8 + tm, :] = proj(5) * proj(6)
    w = conv_ref[...]
    y = (w[0:1, :] * z_scr[6:6 + tm, :] + w[1:2, :] * z_scr[7:7 + tm, :] + w[2:3, :] * z_scr[8:8 + tm, :])
    z_scr[0:8, :] = z_scr[tm:tm + 8, :]
    parts.append((proj(4) * y).astype(BF16))

    o_ref[0] = x + _dot(jnp.concatenate(parts, axis=-1), wout_ref[...])


def _even_mixer(x, lbp, g, w_in, a_out_norm, b_conv, w_out, *, tm):
    bn, s, d = x.shape
    sums, pair = _hgrn_tables()
    n_in = w_in.shape[1]
    return pl.pallas_call(
        _even_kernel,
        grid=(bn, s // tm),
        in_specs=[pl.BlockSpec((1, tm, d), lambda b, i: (b, i, 0)),
                  _const_spec(lbp.shape), _const_spec((1, d)), _const_spec((d, n_in)),
                  _const_spec(a_out_norm.shape), _const_spec(b_conv.shape),
                  _const_spec(w_out.shape), _const_spec(sums.shape), _const_spec(pair.shape)],
        out_specs=pl.BlockSpec((1, tm, d), lambda b, i: (b, i, 0)),
        out_shape=jax.ShapeDtypeStruct(x.shape, F32),
        scratch_shapes=[pltpu.VMEM((tm, A_WIDTH), F32), pltpu.VMEM((tm, A_WIDTH), F32),
                        pltpu.VMEM((tm, A_WIDTH), F32), pltpu.VMEM((tm, A_WIDTH), BF16),
                        pltpu.VMEM((tm, A_WIDTH), BF16), pltpu.VMEM((tm, A_WIDTH), F32),
                        pltpu.VMEM((tm + 8, B_WIDTH), F32),
                        pltpu.VMEM((A_HEADS, A_VDIM, A_KDIM), F32)],
        compiler_params=_params(),
        name="even_mixer",
    )(x, lbp, g.reshape(1, d), w_in.astype(BF16), a_out_norm, b_conv, w_out.astype(BF16),
      jnp.asarray(sums, BF16), jnp.asarray(pair, F32))


def _odd_proj_kernel(x_ref, g_ref, win_ref, wwt_ref, qg_ref, kvg_ref, wuq_ref, wukt_ref, iwqt_ref, ikg_ref, ikb_ref,
                     vg_ref, vb_ref, ws_ref, bs_ref,
                     qlat_t_ref, qidx_t_ref, widx_t_ref, ckv_ref, ckv_t_ref, kidx_ref, od_ref):
    tm = x_ref.shape[1]
    tq = DSA_QUERIES
    kc = ckv_t_ref.shape[-1]
    h = _rms(x_ref[0], g_ref[...]).astype(BF16)
    o = 0

    def proj(width):
        nonlocal o
        r = _dot(h, win_ref[:, o:o + width])
        o += width
        return r

    cq = _rms(proj(C_Q_RANK), qg_ref[...]).astype(BF16)
    ckv = _rms(proj(C_KV_RANK), kvg_ref[...])
    ckv_ref[0] = ckv.astype(BF16)
    for j in range(tm // kc):
        ckv_t_ref[0, j, :C_KV_RANK, :] = ckv[j * kc:(j + 1) * kc, :].T.astype(BF16)
        ckv_t_ref[0, j, C_KV_RANK:, :] = jnp.ones((DSA_SUM_ROWS, kc), BF16)
    du = proj(D_WIDTH)
    dv = proj(D_WIDTH)
    kidx_ref[0] = _layer_norm(proj(LANES)[:, :C_IDX_DIM], ikg_ref[...], ikb_ref[...]).astype(BF16)
    widx_t_ref[0] = _dot_nt(wwt_ref[...], h)[:C_IDX_HEADS, :] * (C_IDX_HEADS ** -0.5)

    q = _dot(cq, wuq_ref[...])
    for i in range(C_HEADS):
        qh = q[:, i * C_HEAD_DIM:(i + 1) * C_HEAD_DIM].astype(BF16)
        lat_t = (_dot_nt(wukt_ref[i], qh) * (C_HEAD_DIM ** -0.5 * LOG2E)).astype(BF16)
        idx_t = (_dot_nt(iwqt_ref[i], cq) * (C_IDX_DIM ** -0.5)).astype(BF16)
        g, k = divmod(i, DSA_GROUP_HEADS)
        for j in range(tm // tq):
            qlat_t_ref[0, j, g, :, k * tq:(k + 1) * tq] = lat_t[:, j * tq:(j + 1) * tq]
            qidx_t_ref[0, j, :, i * tq:(i + 1) * tq] = idx_t[:, j * tq:(j + 1) * tq]

    u = _gelu(du)
    v = _layer_norm(_gelu(dv), vg_ref[...], vb_ref[...]).astype(BF16)
    tri = (lax.broadcasted_iota(I32, (D_CHUNK, D_CHUNK), 0) >= lax.broadcasted_iota(I32, (D_CHUNK, D_CHUNK), 1))
    gw = D_WIDTH // D_GROUPS
    for gi in range(D_GROUPS):
        wc = jnp.where(tri, ws_ref[gi], 0.0).astype(BF16)
        cols = slice(gi * gw, (gi + 1) * gw)
        for c0 in range(0, tm, D_CHUNK):
            mixed = _dot(wc, v[c0:c0 + D_CHUNK, cols]) + bs_ref[:, gi:gi + 1]
            od_ref[0, c0:c0 + D_CHUNK, cols] = (u[c0:c0 + D_CHUNK, cols] * mixed).astype(BF16)


def _odd_proj(x, g, w_in, q_g, kv_g, w_uq, w_uk, idx_wq, ik_g, ik_b, v_g, v_b, w_s, b_s, *, tm):
    bn, s, d = x.shape
    c0, c1, c2, c3 = C_Q_RANK, C_Q_RANK + C_KV_RANK, C_Q_RANK + C_KV_RANK + C_IDX_DIM, \
        C_Q_RANK + C_KV_RANK + C_IDX_DIM + C_IDX_HEADS
    pad = LANES - C_IDX_DIM
    w = jnp.concatenate([w_in[:, :c1], w_in[:, c3:], w_in[:, c1:c2], jnp.zeros((d, pad), w_in.dtype)],
                        axis=1).astype(BF16)
    ww_t = jnp.concatenate([w_in[:, c2:c3].T, jnp.zeros((16 - C_IDX_HEADS, d), w_in.dtype)], axis=0).astype(BF16)
    wuk_t = jnp.transpose(w_uk, (0, 2, 1)).astype(BF16)
    iwq_t = jnp.transpose(idx_wq.reshape(C_Q_RANK, C_IDX_HEADS, C_IDX_DIM), (1, 2, 0)).astype(BF16)
    kc = min(DSA_KEY_CHUNK, s)
    assert tm % kc == 0 and tm % DSA_QUERIES == 0
    row = lambda a: a.reshape(1, -1)
    shp = lambda n, dt: jax.ShapeDtypeStruct((bn, s, n), dt)
    blk = lambda n: pl.BlockSpec((1, tm, n), lambda b, i: (b, i, 0))
    nq = s // DSA_QUERIES
    heads_q = C_HEADS * DSA_QUERIES
    n_groups, group_q = C_HEADS // DSA_GROUP_HEADS, DSA_GROUP_HEADS * DSA_QUERIES
    qblk = lambda n: pl.BlockSpec((1, tm // DSA_QUERIES, n, heads_q), lambda b, i: (b, i, 0, 0))
    return pl.pallas_call(
        _odd_proj_kernel,
        grid=(bn, s // tm),
        in_specs=[blk(d), _const_spec((1, d)), _const_spec(w.shape), _const_spec(ww_t.shape),
                  _const_spec((1, C_Q_RANK)), _const_spec((1, C_KV_RANK)),
                  _const_spec(w_uq.shape), _const_spec(wuk_t.shape), _const_spec(iwq_t.shape),
                  _const_spec((1, C_IDX_DIM)), _const_spec((1, C_IDX_DIM)),
                  _const_spec((1, D_WIDTH)), _const_spec((1, D_WIDTH)),
                  _const_spec(w_s.shape), _const_spec((D_CHUNK, D_GROUPS))],
        out_specs=[pl.BlockSpec((1, tm // DSA_QUERIES, n_groups, C_KV_RANK, group_q), lambda b, i: (b, i, 0, 0, 0)),
                   qblk(C_IDX_DIM),
                   pl.BlockSpec((1, C_IDX_HEADS, tm), lambda b, i: (b, 0, i)),
                   blk(C_KV_RANK),
                   pl.BlockSpec((1, tm // kc, C_KV_RANK + DSA_SUM_ROWS, kc), lambda b, i: (b, i, 0, 0)),
                   blk(C_IDX_DIM), blk(D_WIDTH)],
        out_shape=[jax.ShapeDtypeStruct((bn, nq, n_groups, C_KV_RANK, group_q), BF16),
                   jax.ShapeDtypeStruct((bn, nq, C_IDX_DIM, heads_q), BF16),
                   jax.ShapeDtypeStruct((bn, C_IDX_HEADS, s), F32),
                   shp(C_KV_RANK, BF16),
                   jax.ShapeDtypeStruct((bn, s // kc, C_KV_RANK + DSA_SUM_ROWS, kc), BF16),
                   shp(C_IDX_DIM, BF16), shp(D_WIDTH, BF16)],
        compiler_params=_params(),
        name="odd_proj",
    )(x, row(g), w, ww_t, row(q_g), row(kv_g), w_uq.astype(BF16), wuk_t, iwq_t,
      row(ik_g), row(ik_b), row(v_g), row(v_b), w_s, b_s.T)


def _order_key(score):
    bits = lax.bitcast_convert_type(score + 0.0, I32)
    return bits ^ ((bits >> 31) & 0x7FFFFFFF)


def _dsa_kernel(qlat_t_ref, qidx_t_ref, widx_t_ref, kidx_ref, ckv_ref, ckv_t_ref, wuv_ref,
                o_ref, key_scr, s_scr, mx_scr, m_scr, acc_scr, *, topk):
    tq = DSA_QUERIES
    kc = key_scr.shape[1]
    i = pl.program_id(1)
    n_chunks = ((i + 1) * tq + kc - 1) // kc
    t_pos = i * tq + lax.broadcasted_iota(I32, (1, tq), 1)

    qidx_t = qidx_t_ref[0, 0]
    widx_t = widx_t_ref[0]

    def score_chunk(c, carry):
        kb = kidx_ref[0, pl.ds(pl.multiple_of(c * kc, kc), kc), :]
        logits = _dot(kb, qidx_t)
        score = jnp.zeros((kc, tq), F32)
        for hh in range(C_IDX_HEADS):
            score = score + jnp.maximum(logits[:, hh * tq:(hh + 1) * tq], 0.0) * widx_t[hh:hh + 1, :]
        key_pos = c * kc + lax.broadcasted_iota(I32, (kc, 1), 0)
        key_scr[c] = jnp.where(key_pos <= t_pos, _order_key(score), INT_MIN)
        return carry

    lax.fori_loop(0, n_chunks, score_chunk, 0)

    def count_ge(thr):
        fold = 4 * 8
        def body(c, acc):
            for r in range(0, kc, fold):
                acc = acc + jnp.where(key_scr[c, r:r + fold, :] >= thr, 1.0, 0.0)
            return acc
        acc = lax.fori_loop(0, n_chunks, body, jnp.zeros((fold, tq), F32))
        return jnp.sum(acc, axis=0, keepdims=True)

    def unresolved(state):
        it, _, cnt = state
        return jnp.logical_and(it < 32, jnp.max(cnt) > topk)

    def bit_steps(state):
        it, prefix, cnt = state
        for _ in range(DSA_BITS_PER_CHECK):
            trial = prefix | (jnp.int32(1) << (31 - it))
            c = count_ge(trial ^ INT_MIN)
            keep = c >= topk
            it, prefix, cnt = it + 1, jnp.where(keep, trial, prefix), jnp.where(keep, c, cnt)
        return it, prefix, cnt

    n_causal = (t_pos + 1).astype(F32)
    _, prefix, n_ge = lax.while_loop(unresolved, bit_steps, (jnp.int32(0), jnp.zeros((1, tq), I32), n_causal))
    thr = jnp.maximum(prefix ^ INT_MIN, INT_MIN + 1)

    has_surplus = n_ge > topk

    @pl.when(jnp.max(n_ge) > topk)
    def _():
        n_gt = count_ge(thr + 1)
        need = jnp.where(has_surplus, topk - n_gt, float(kc) * 1e6)
        lower = (lax.broadcasted_iota(I32, (kc, kc), 0) >= lax.broadcasted_iota(I32, (kc, kc), 1))
        lower = jnp.where(lower, 1.0, 0.0).astype(BF16)

        def body(c, seen):
            keys = key_scr[c]
            eq = keys == thr
            rank = seen + _dot(lower, jnp.where(eq, 1.0, 0.0).astype(BF16))
            key_scr[c] = jnp.where(jnp.logical_and(eq, rank > need), INT_MIN, keys)
            return rank[kc - 1:kc, :]

        lax.fori_loop(0, n_chunks, body, jnp.zeros((1, tq), F32))

    n_groups = s_scr.shape[1]
    m_scr[...] = jnp.full(m_scr.shape, -jnp.inf, F32)
    acc_scr[...] = jnp.zeros(acc_scr.shape, F32)

    def pipeline_step(c_next, c_now):
        if c_next is not None:
            c_next, slot_next = c_next
            kv = ckv_ref[0, pl.ds(pl.multiple_of(c_next * kc, kc), kc), :]
            bias = jnp.where(key_scr[c_next] >= thr, 0.0, -jnp.inf)
            bias = jnp.concatenate([bias] * DSA_GROUP_HEADS, axis=1)
        if c_now is not None:
            c_now, slot_now = c_now
            kv_t = ckv_t_ref[0, c_now]
        for g in range(n_groups):
            if c_next is not None:
                sb = _dot(kv, qlat_t_ref[0, 0, g]) + bias
                s_scr[slot_next, g] = sb
                mx_scr[slot_next, g] = jnp.max(sb, axis=0, keepdims=True)
            if c_now is not None:
                m_old = m_scr[g]
                m_new = jnp.maximum(m_old, mx_scr[slot_now, g])
                m_safe = jnp.where(m_new == -jnp.inf, 0.0, m_new)
                alpha = jnp.exp2(m_old - m_safe)
                p = jnp.exp2(s_scr[slot_now, g] - m_safe)
                m_scr[g] = m_new
                acc_scr[g] = alpha * acc_scr[g] + _dot(kv_t, p.astype(BF16))

    pipeline_step((jnp.int32(0), 0), None)
    n_pairs = (n_chunks - 1) // 2

    def attn_pair(j, carry):
        c = 2 * j
        pipeline_step((c + 1, 1), (c, 0))
        pipeline_step((c + 2, 0), (c + 1, 1))
        return carry

    lax.fori_loop(0, n_pairs, attn_pair, 0)
    last = n_chunks - 1

    @pl.when(last == 2 * n_pairs)
    def _():
        pipeline_step(None, (last, 0))

    @pl.when(last != 2 * n_pairs)
    def _():
        pipeline_step((last, 1), (last - 1, 0))
        pipeline_step(None, (last, 1))
    outs = []
    for hh in range(C_HEADS):
        g, k = divmod(hh, DSA_GROUP_HEADS)
        cols = slice(k * tq, (k + 1) * tq)
        o_lat_t = acc_scr[g, :C_KV_RANK, cols] / acc_scr[g, C_KV_RANK:C_KV_RANK + 1, cols]
        outs.append(_dot_tn(o_lat_t.astype(BF16), wuv_ref[hh]).astype(BF16))
    o_ref[0] = jnp.concatenate(outs, axis=-1)


def _dsa(q_lat_t, q_idx_t, w_idx_t, k_idx, ckv, ckv_t, w_uv):
    bn, s, _ = ckv.shape
    tq = DSA_QUERIES
    kc = ckv_t.shape[-1]
    topk = min(C_MAX_TOPK, s // 4)
    heads_q = C_HEADS * tq
    n_groups, group_q = q_lat_t.shape[2], q_lat_t.shape[4]
    full = lambda n: pl.BlockSpec((1, s, n), lambda b, i: (b, 0, 0))
    width = C_HEADS * C_HEAD_DIM
    return pl.pallas_call(
        functools.partial(_dsa_kernel, topk=float(topk)),
        grid=(bn, s // tq),
        in_specs=[pl.BlockSpec((1, 1, n_groups, C_KV_RANK, group_q), lambda b, i: (b, i, 0, 0, 0)),
                  pl.BlockSpec((1, 1, C_IDX_DIM, heads_q), lambda b, i: (b, i, 0, 0)),
                  pl.BlockSpec((1, C_IDX_HEADS, tq), lambda b, i: (b, 0, i)),
                  full(C_IDX_DIM), full(C_KV_RANK),
                  pl.BlockSpec((1, s // kc, ckv_t.shape[2], kc), lambda b, i: (b, 0, 0, 0)),
                  _const_spec(w_uv.shape)],
        out_specs=pl.BlockSpec((1, tq, width), lambda b, i: (b, i, 0)),
        out_shape=jax.ShapeDtypeStruct((bn, s, width), BF16),
        scratch_shapes=[pltpu.VMEM((s // kc, kc, tq), I32), pltpu.VMEM((2, n_groups, kc, group_q), F32),
                        pltpu.VMEM((2, n_groups, 1, group_q), F32),
                        pltpu.VMEM((n_groups, 1, group_q), F32),
                        pltpu.VMEM((n_groups, ckv_t.shape[2], group_q), F32)],
        compiler_params=_params(),
        name="dsa",
    )(q_lat_t, q_idx_t, w_idx_t, k_idx, ckv, ckv_t, w_uv.astype(BF16))


def _row_tile(s, want):
    return want if s % want == 0 else s


def kernel(x, mem, hgrn_lower_bounds, even_mix_norm, even_w_in, even_a_out_norm, even_b_conv, even_w_out, odd_mix_norm, odd_w_in, odd_c_q_norm, odd_c_kv_norm, odd_c_w_uq, odd_c_w_uk, odd_c_w_uv, odd_c_idx_wq, odd_c_idx_k_g, odd_c_idx_k_b, odd_d_v_g, odd_d_v_b, odd_d_w_s, odd_d_b_s, odd_w_out, xa_norm, xa_mem_norm, xa_wq, xa_wkv, xa_wo, ffn_norm, ffn_w13, ffn_w2, final_norm):
    s = x.shape[1]
    tm = _row_tile(s, 512)
    k_mem, v_mem = _mem_kv(mem, xa_mem_norm, xa_wkv)

    def xattn_ffn(x, layer, last, **mix):
        return _xattn_ffn(x, k_mem[layer], v_mem[layer], xa_norm[layer], xa_wq[layer], xa_wo[layer],
                          ffn_norm[layer], ffn_w13[layer], ffn_w2[layer], final_norm,
                          final_norm=last, tm=tm, **mix)

    x = _even_mixer(x, hgrn_lower_bounds, even_mix_norm[0], even_w_in[0], even_a_out_norm[0],
                    even_b_conv[0], even_w_out[0], tm=tm)
    x = xattn_ffn(x, 0, False)
    q_lat_t, q_idx_t, w_idx_t, ckv, ckv_t, k_idx, o_d = _odd_proj(
        x, odd_mix_norm[0], odd_w_in[0], odd_c_q_norm[0], odd_c_kv_norm[0], odd_c_w_uq[0], odd_c_w_uk[0],
        odd_c_idx_wq[0], odd_c_idx_k_g[0], odd_c_idx_k_b[0], odd_d_v_g[0], odd_d_v_b[0],
        odd_d_w_s[0], odd_d_b_s[0], tm=tm)
    o_c = _dsa(q_lat_t, q_idx_t, w_idx_t, k_idx, ckv, ckv_t, odd_c_w_uv[0])
    return xattn_ffn(x, 1, True, mix=(o_c, o_d), w_mix=odd_w_out[0])
```

```python
import functools

import jax
import jax.numpy as jnp
import numpy as np
from jax import lax
from jax.experimental import pallas as pl
from jax.experimental.pallas import tpu as pltpu

F32 = jnp.float32
BF16 = jnp.bfloat16
I32 = jnp.int32

V7X_VMEM_BYTES = 64 * 1024 * 1024
VMEM_LIMIT = 56 * 1024 * 1024
LANES = 128

A_HEADS, A_KDIM, A_VDIM, A_CHUNK = 4, 128, 128, 64
A_WIDTH = A_HEADS * A_VDIM
B_WIDTH = 512
C_HEADS, C_HEAD_DIM, C_Q_RANK, C_KV_RANK = 8, 64, 256, 128
C_IDX_HEADS, C_IDX_DIM, C_MAX_TOPK = 8, 64, 256
D_GROUPS, D_CHUNK, D_WIDTH = 4, 128, 512
X_HEADS = 4
HGRN_LEVELS = 6
HGRN_DIRECT_MAX_DECAY = 64.0
DSA_KEY_CHUNK = 512
SLAB_ROWS = 32 * 8
DSA_SUM_ROWS = 16
DSA_QUERIES = 256
DSA_GROUP_HEADS = 1
LOG2E = 1.4426950408889634
INT_MIN = -(2 ** 31)


def _dot(a, b):
    return jnp.dot(a, b, preferred_element_type=F32)


def _dot_nt(a, b):
    return lax.dot_general(a, b, (((1,), (1,)), ((), ())), preferred_element_type=F32)


def _dot_tn(a, b):
    return lax.dot_general(a, b, (((0,), (0,)), ((), ())), preferred_element_type=F32)


def _rms(x, g, eps=1e-6):
    return x * lax.rsqrt(jnp.mean(x * x, axis=-1, keepdims=True) + eps) * g


def _layer_norm(x, g, b, eps=1e-5):
    mu = jnp.mean(x, axis=-1, keepdims=True)
    xc = x - mu
    var = jnp.mean(xc * xc, axis=-1, keepdims=True)
    return xc * lax.rsqrt(var + eps) * g + b


def _silu(x):
    return x * jax.nn.sigmoid(x)


def _gelu(x):
    return 0.5 * x * (1.0 + lax.erf(x * (2.0 ** -0.5)))


def _const_spec(shape):
    zeros = (0,) * len(shape)
    return pl.BlockSpec(shape, lambda *_: zeros, pipeline_mode=pl.Buffered(1))


def _params(n_parallel=0, n_arbitrary=2):
    return pltpu.CompilerParams(
        dimension_semantics=("parallel",) * n_parallel + ("arbitrary",) * n_arbitrary,
        vmem_limit_bytes=VMEM_LIMIT)


def _mem_kv_kernel(mem_ref, g_ref, wkv_ref, k_ref, v_ref):
    d = mem_ref.shape[-1]
    m = _rms(mem_ref[0], g_ref[0]).astype(BF16)
    kv = _dot(m, wkv_ref[0])
    k_ref[0, 0] = kv[:, :d].astype(BF16)
    v_ref[0, 0] = kv[:, d:].astype(BF16)


def _mem_kv(mem, mem_g, wkv):
    depth, d = mem_g.shape
    bn, m_len, _ = mem.shape
    out = jax.ShapeDtypeStruct((depth, bn, m_len, d), BF16)
    return pl.pallas_call(
        _mem_kv_kernel,
        grid=(depth, bn),
        in_specs=[pl.BlockSpec((1, m_len, d), lambda l, b: (b, 0, 0)),
                  pl.BlockSpec((1, 1, d), lambda l, b: (l, 0, 0)),
                  pl.BlockSpec((1, d, 2 * d), lambda l, b: (l, 0, 0))],
        out_specs=[pl.BlockSpec((1, 1, m_len, d), lambda l, b: (l, b, 0, 0))] * 2,
        out_shape=[out, out],
        compiler_params=_params(),
        name="mem_kv",
    )(mem, mem_g.reshape(depth, 1, d), wkv.astype(BF16))


def _xattn_ffn_kernel(*refs, final_norm, ff_chunk, mix_parts):
    n_mix = mix_parts + 1 if mix_parts else 0
    mix_refs, refs = refs[:n_mix], refs[n_mix:]
    x_ref, k_ref, v_ref, gxa_ref, wq_ref, wo_ref, gff_ref, w13_ref, w2_ref, gfin_ref, o_ref = refs
    x = x_ref[0]
    if mix_parts:
        mixed = jnp.concatenate([r[0] for r in mix_refs[:mix_parts]], axis=-1)
        x = x + _dot(mixed, mix_refs[mix_parts][...])
    d = x.shape[-1]
    hd = d // X_HEADS
    ff = w2_ref.shape[0]
    h = _rms(x, gxa_ref[...]).astype(BF16)
    q = _dot(h, wq_ref[...])
    heads = []
    for i in range(X_HEADS):
        cols = slice(i * hd, (i + 1) * hd)
        sc = _dot_nt(q[:, cols].astype(BF16), k_ref[0, :, cols]) * (hd ** -0.5)
        p = jnp.exp(sc - jnp.max(sc, axis=-1, keepdims=True))
        o = _dot(p.astype(BF16), v_ref[0, :, cols])
        heads.append((o / jnp.sum(p, axis=-1, keepdims=True)).astype(BF16))
    x = x + _dot(jnp.concatenate(heads, axis=-1), wo_ref[...])
    h2 = _rms(x, gff_ref[...]).astype(BF16)
    acc = jnp.zeros_like(x)
    for c0 in range(0, ff, ff_chunk):
        g = _dot(h2, w13_ref[:, c0:c0 + ff_chunk])
        u = _dot(h2, w13_ref[:, ff + c0:ff + c0 + ff_chunk])
        acc = acc + _dot((_silu(g) * u).astype(BF16), w2_ref[c0:c0 + ff_chunk, :])
    x = x + acc
    if final_norm:
        x = _rms(x, gfin_ref[...])
    o_ref[0] = x


def _xattn_ffn(x, k_mem, v_mem, g_xa, wq, wo, g_ff, w13, w2, g_fin, *, final_norm, tm, mix=(), w_mix=None):
    bn, s, d = x.shape
    mix_specs = [pl.BlockSpec((1, tm, a.shape[-1]), lambda b, i: (b, i, 0)) for a in mix]
    mix_args = list(mix)
    if mix:
        mix_specs.append(_const_spec(w_mix.shape))
        mix_args.append(w_mix.astype(BF16))
    m_len = k_mem.shape[1]
    ff = w2.shape[0]
    ff_chunk = ff // 2 if (ff // 2) % LANES == 0 else ff
    row = lambda a: a.reshape(1, d)
    kern = functools.partial(_xattn_ffn_kernel, final_norm=final_norm, ff_chunk=ff_chunk, mix_parts=len(mix))
    return pl.pallas_call(
        kern,
        grid=(bn, s // tm),
        in_specs=mix_specs + [
                  pl.BlockSpec((1, tm, d), lambda b, i: (b, i, 0)),
                  pl.BlockSpec((1, m_len, d), lambda b, i: (b, 0, 0)),
                  pl.BlockSpec((1, m_len, d), lambda b, i: (b, 0, 0)),
                  _const_spec((1, d)), _const_spec((d, d)), _const_spec((d, d)),
                  _const_spec((1, d)), _const_spec((d, 2 * ff)), _const_spec((ff, d)),
                  _const_spec((1, d))],
        out_specs=pl.BlockSpec((1, tm, d), lambda b, i: (b, i, 0)),
        out_shape=jax.ShapeDtypeStruct(x.shape, F32),
        compiler_params=_params(),
        name="xattn_ffn_final" if final_norm else "xattn_ffn",
    )(*mix_args, x, k_mem, v_mem, row(g_xa), wq.astype(BF16), wo.astype(BF16), row(g_ff),
      w13.astype(BF16), w2.astype(BF16), row(g_fin))


def _hgrn_tables():
    n = A_CHUNK
    t = np.arange(n)[:, None]
    u = np.arange(n)[None, :]
    sums, pair = [], []
    for lvl in range(HGRN_LEVELS):
        c = n >> (lvl + 1)
        mid = (t // (2 * c)) * (2 * c) + c
        late = t >= mid
        sums.append(np.where(late, (u >= mid) & (u <= t), (u > t) & (u < mid)))
        pair.append(late & (u < mid) & (u >= mid - c))
    sums.append(u <= t)
    sums.append(u > t)
    pair.append(t == u)
    return (np.concatenate(sums, axis=0).astype(np.float32),
            np.stack(pair, axis=0).astype(np.float32))


def _even_kernel(x_ref, lbp_ref, g_ref, win_ref, aon_ref, conv_ref, wout_ref, sums_ref, pair_ref,
                 o_ref, q_scr, k_scr, v_scr, lfh_scr, lfl_scr, oa_scr, z_scr, st_scr):
    tm = x_ref.shape[1]
    n = A_CHUNK

    @pl.when(pl.program_id(1) == 0)
    def _():
        st_scr[...] = jnp.zeros_like(st_scr)
        z_scr[0:8, :] = jnp.zeros((8, B_WIDTH), F32)

    x = x_ref[0]
    h = _rms(x, g_ref[...]).astype(BF16)

    def proj(j, width=A_WIDTH):
        return _dot(h, win_ref[:, j * width:(j + 1) * width])

    lbp = lbp_ref[...]
    e = jnp.exp(lbp - jnp.max(lbp, axis=0, keepdims=True))
    lb = e[0:1, :] / jnp.sum(e, axis=0, keepdims=True)

    q_scr[...] = _silu(proj(0))
    f = lb + (1.0 - lb) * jax.nn.sigmoid(proj(1))
    k_scr[...] = 1.0 - f
    logf = jnp.log(f)
    lf_hi = logf.astype(BF16)
    lfh_scr[...] = lf_hi
    lfl_scr[...] = (logf - lf_hi.astype(F32)).astype(BF16)
    v_scr[...] = proj(2)

    row = lax.broadcasted_iota(I32, (n, 1), 0)
    causal = jnp.where(row >= lax.broadcasted_iota(I32, (1, n), 1), 1.0, 0.0)

    def chunk(c, carry, direct):
        r0 = pl.multiple_of(c * n, n)
        rows = pl.ds(r0, n)
        lf_hi, lf_lo = lfh_scr[rows, :], lfl_scr[rows, :]
        qc, kc, vc = q_scr[rows, :], k_scr[rows, :], v_scr[rows, :]
        edge = sums_ref[HGRN_LEVELS * n:, :]
        d_edge = _dot(edge, lf_hi) + _dot(edge, lf_lo)
        b = d_edge[:n]
        e_in, e_out = jnp.exp(b), jnp.exp(d_edge[n:])
        if direct:
            e_neg = jnp.exp(-b)
        else:
            tree = sums_ref[:HGRN_LEVELS * n, :]
            ex = jnp.exp(_dot(tree, lf_hi) + _dot(tree, lf_lo))
        for hh in range(A_HEADS):
            cols = slice(hh * A_KDIM, (hh + 1) * A_KDIM)
            qh, kh = qc[:, cols], kc[:, cols]
            qin = (qh * e_in[:, cols]).astype(BF16)
            if direct:
                attn = _dot_nt(qin, (kh * e_neg[:, cols]).astype(BF16)) * causal
            else:
                attn = _dot_nt(qh.astype(BF16), kh.astype(BF16)) * pair_ref[HGRN_LEVELS]
                for lvl in range(HGRN_LEVELS):
                    late = (row & (n >> (lvl + 1))) != 0
                    xl = (jnp.where(late, qh, kh) * ex[lvl * n:(lvl + 1) * n, cols]).astype(BF16)
                    attn = attn + _dot_nt(xl, xl) * pair_ref[lvl]
            st = st_scr[hh]
            vh = vc[:, cols].astype(BF16)
            oa_scr[rows, cols] = _dot(attn.astype(BF16), vh) + _dot_nt(qin, st.astype(BF16))
            st_scr[hh] = st * e_in[n - 1:n, cols] + _dot_tn(vh, (kh * e_out[:, cols]).astype(BF16))
        return carry

    decay = -jnp.sum(logf.reshape(tm // n, n, A_WIDTH), axis=1)
    mild = jnp.max(decay) <= HGRN_DIRECT_MAX_DECAY

    @pl.when(mild)
    def _():
        lax.fori_loop(0, tm // n, functools.partial(chunk, direct=True), 0, unroll=2)

    @pl.when(jnp.logical_not(mild))
    def _():
        lax.fori_loop(0, tm // n, functools.partial(chunk, direct=False), 0, unroll=2)

    gate = _silu(proj(3))
    parts = []
    for hh in range(A_HEADS):
        cols = slice(hh * A_VDIM, (hh + 1) * A_VDIM)
        parts.append((_rms(oa_scr[:, cols], aon_ref[hh:hh + 1, :]) * gate[:, cols]).astype(BF16))

    z_scr[8:8 + tm, :] = proj(5) * proj(6)
    w = conv_ref[...]
    y = (w[0:1, :] * z_scr[6:6 + tm, :] + w[1:2, :] * z_scr[7:7 + tm, :] + w[2:3, :] * z_scr[8:8 + tm, :])
    z_scr[0:8, :] = z_scr[tm:tm + 8, :]
    parts.append((proj(4) * y).astype(BF16))

    o_ref[0] = x + _dot(jnp.concatenate(parts, axis=-1), wout_ref[...])


def _even_mixer(x, lbp, g, w_in, a_out_norm, b_conv, w_out, *, tm):
    bn, s, d = x.shape
    sums, pair = _hgrn_tables()
    n_in = w_in.shape[1]
    return pl.pallas_call(
        _even_kernel,
        grid=(bn, s // tm),
        in_specs=[pl.BlockSpec((1, tm, d), lambda b, i: (b, i, 0)),
                  _const_spec(lbp.shape), _const_spec((1, d)), _const_spec((d, n_in)),
                  _const_spec(a_out_norm.shape), _const_spec(b_conv.shape),
                  _const_spec(w_out.shape), _const_spec(sums.shape), _const_spec(pair.shape)],
        out_specs=pl.BlockSpec((1, tm, d), lambda b, i: (b, i, 0)),
        out_shape=jax.ShapeDtypeStruct(x.shape, F32),
        scratch_shapes=[pltpu.VMEM((tm, A_WIDTH), F32), pltpu.VMEM((tm, A_WIDTH), F32),
                        pltpu.VMEM((tm, A_WIDTH), F32), pltpu.VMEM((tm, A_WIDTH), BF16),
                        pltpu.VMEM((tm, A_WIDTH), BF16), pltpu.VMEM((tm, A_WIDTH), F32),
                        pltpu.VMEM((tm + 8, B_WIDTH), F32),
                        pltpu.VMEM((A_HEADS, A_VDIM, A_KDIM), F32)],
        compiler_params=_params(),
        name="even_mixer",
    )(x, lbp, g.reshape(1, d), w_in.astype(BF16), a_out_norm, b_conv, w_out.astype(BF16),
      jnp.asarray(sums, BF16), jnp.asarray(pair, F32))


def _odd_proj_kernel(x_ref, g_ref, win_ref, wwt_ref, qg_ref, kvg_ref, wuq_ref, wukt_ref, iwqt_ref, ikg_ref, ikb_ref,
                     vg_ref, vb_ref, ws_ref, bs_ref,
                     qlat_t_ref, qidx_t_ref, widx_t_ref, ckv_ref, ckv_t_ref, kidx_ref, od_ref):
    tm = x_ref.shape[1]
    tq = DSA_QUERIES
    kc = ckv_t_ref.shape[-1]
    h = _rms(x_ref[0], g_ref[...]).astype(BF16)
    o = 0

    def proj(width):
        nonlocal o
        r = _dot(h, win_ref[:, o:o + width])
        o += width
        return r

    cq = _rms(proj(C_Q_RANK), qg_ref[...]).astype(BF16)
    ckv = _rms(proj(C_KV_RANK), kvg_ref[...])
    ckv_ref[0] = ckv.astype(BF16)
    for j in range(tm // kc):
        ckv_t_ref[0, j, :C_KV_RANK, :] = ckv[j * kc:(j + 1) * kc, :].T.astype(BF16)
        ckv_t_ref[0, j, C_KV_RANK:, :] = jnp.ones((DSA_SUM_ROWS, kc), BF16)
    du = proj(D_WIDTH)
    dv = proj(D_WIDTH)
    kidx_ref[0] = _layer_norm(proj(LANES)[:, :C_IDX_DIM], ikg_ref[...], ikb_ref[...]).astype(BF16)
    widx_t_ref[0] = _dot_nt(wwt_ref[...], h)[:C_IDX_HEADS, :] * (C_IDX_HEADS ** -0.5)

    q = _dot(cq, wuq_ref[...])
    for i in range(C_HEADS):
        qh = q[:, i * C_HEAD_DIM:(i + 1) * C_HEAD_DIM].astype(BF16)
        lat_t = (_dot_nt(wukt_ref[i], qh) * (C_HEAD_DIM ** -0.5 * LOG2E)).astype(BF16)
        idx_t = (_dot_nt(iwqt_ref[i], cq) * (C_IDX_DIM ** -0.5)).astype(BF16)
        g, k = divmod(i, DSA_GROUP_HEADS)
        for j in range(tm // tq):
            qlat_t_ref[0, j, g, :, k * tq:(k + 1) * tq] = lat_t[:, j * tq:(j + 1) * tq]
            qidx_t_ref[0, j, :, i * tq:(i + 1) * tq] = idx_t[:, j * tq:(j + 1) * tq]

    u = _gelu(du)
    v = _layer_norm(_gelu(dv), vg_ref[...], vb_ref[...]).astype(BF16)
    tri = (lax.broadcasted_iota(I32, (D_CHUNK, D_CHUNK), 0) >= lax.broadcasted_iota(I32, (D_CHUNK, D_CHUNK), 1))
    gw = D_WIDTH // D_GROUPS
    for gi in range(D_GROUPS):
        wc = jnp.where(tri, ws_ref[gi], 0.0).astype(BF16)
        cols = slice(gi * gw, (gi + 1) * gw)
        for c0 in range(0, tm, D_CHUNK):
            mixed = _dot(wc, v[c0:c0 + D_CHUNK, cols]) + bs_ref[:, gi:gi + 1]
            od_ref[0, c0:c0 + D_CHUNK, cols] = (u[c0:c0 + D_CHUNK, cols] * mixed).astype(BF16)


def _odd_proj(x, g, w_in, q_g, kv_g, w_uq, w_uk, idx_wq, ik_g, ik_b, v_g, v_b, w_s, b_s, *, tm):
    bn, s, d = x.shape
    c0, c1, c2, c3 = C_Q_RANK, C_Q_RANK + C_KV_RANK, C_Q_RANK + C_KV_RANK + C_IDX_DIM, \
        C_Q_RANK + C_KV_RANK + C_IDX_DIM + C_IDX_HEADS
    pad = LANES - C_IDX_DIM
    w = jnp.concatenate([w_in[:, :c1], w_in[:, c3:], w_in[:, c1:c2], jnp.zeros((d, pad), w_in.dtype)],
                        axis=1).astype(BF16)
    ww_t = jnp.concatenate([w_in[:, c2:c3].T, jnp.zeros((16 - C_IDX_HEADS, d), w_in.dtype)], axis=0).astype(BF16)
    wuk_t = jnp.transpose(w_uk, (0, 2, 1)).astype(BF16)
    iwq_t = jnp.transpose(idx_wq.reshape(C_Q_RANK, C_IDX_HEADS, C_IDX_DIM), (1, 2, 0)).astype(BF16)
    kc = min(DSA_KEY_CHUNK, s)
    assert tm % kc == 0 and tm % DSA_QUERIES == 0
    row = lambda a: a.reshape(1, -1)
    shp = lambda n, dt: jax.ShapeDtypeStruct((bn, s, n), dt)
    blk = lambda n: pl.BlockSpec((1, tm, n), lambda b, i: (b, i, 0))
    nq = s // DSA_QUERIES
    heads_q = C_HEADS * DSA_QUERIES
    n_groups, group_q = C_HEADS // DSA_GROUP_HEADS, DSA_GROUP_HEADS * DSA_QUERIES
    qblk = lambda n: pl.BlockSpec((1, tm // DSA_QUERIES, n, heads_q), lambda b, i: (b, i, 0, 0))
    return pl.pallas_call(
        _odd_proj_kernel,
        grid=(bn, s // tm),
        in_specs=[blk(d), _const_spec((1, d)), _const_spec(w.shape), _const_spec(ww_t.shape),
                  _const_spec((1, C_Q_RANK)), _const_spec((1, C_KV_RANK)),
                  _const_spec(w_uq.shape), _const_spec(wuk_t.shape), _const_spec(iwq_t.shape),
                  _const_spec((1, C_IDX_DIM)), _const_spec((1, C_IDX_DIM)),
                  _const_spec((1, D_WIDTH)), _const_spec((1, D_WIDTH)),
                  _const_spec(w_s.shape), _const_spec((D_CHUNK, D_GROUPS))],
        out_specs=[pl.BlockSpec((1, tm // DSA_QUERIES, n_groups, C_KV_RANK, group_q), lambda b, i: (b, i, 0, 0, 0)),
                   qblk(C_IDX_DIM),
                   pl.BlockSpec((1, C_IDX_HEADS, tm), lambda b, i: (b, 0, i)),
                   blk(C_KV_RANK),
                   pl.BlockSpec((1, tm // kc, C_KV_RANK + DSA_SUM_ROWS, kc), lambda b, i: (b, i, 0, 0)),
                   blk(C_IDX_DIM), blk(D_WIDTH)],
        out_shape=[jax.ShapeDtypeStruct((bn, nq, n_groups, C_KV_RANK, group_q), BF16),
                   jax.ShapeDtypeStruct((bn, nq, C_IDX_DIM, heads_q), BF16),
                   jax.ShapeDtypeStruct((bn, C_IDX_HEADS, s), F32),
                   shp(C_KV_RANK, BF16),
                   jax.ShapeDtypeStruct((bn, s // kc, C_KV_RANK + DSA_SUM_ROWS, kc), BF16),
                   shp(C_IDX_DIM, BF16), shp(D_WIDTH, BF16)],
        compiler_params=_params(),
        name="odd_proj",
    )(x, row(g), w, ww_t, row(q_g), row(kv_g), w_uq.astype(BF16), wuk_t, iwq_t,
      row(ik_g), row(ik_b), row(v_g), row(v_b), w_s, b_s.T)


def _bit_transpose32(words):
    a = list(words)
    j, m = 16, 0x0000FFFF
    while j:
        mask = np.int32(np.uint32(m))
        k = 0
        while k < 32:
            t = (a[k] ^ lax.shift_right_logical(a[k + j], j)) & mask
            a[k] = a[k] ^ t
            a[k + j] = a[k + j] ^ (t << j)
            k = (k + j + 1) & ~j
        j >>= 1
        m = (m ^ (m << j)) & 0xFFFFFFFF
    return a


def _order_key(score):
    bits = lax.bitcast_convert_type(score + 0.0, I32)
    return bits ^ ((bits >> 31) & 0x7FFFFFFF)


def _dsa_kernel(qlat_t_ref, qidx_t_ref, widx_t_ref, kidx_ref, ckv_ref, ckv_t_ref, wuv_ref,
                o_ref, key_scr, plane_scr, cand_scr, s_scr, mx_scr, m_scr, acc_scr, *, topk):
    tq = DSA_QUERIES
    kc = key_scr.shape[1]
    i = pl.program_id(1)
    n_chunks = ((i + 1) * tq + kc - 1) // kc
    t_pos = i * tq + lax.broadcasted_iota(I32, (1, tq), 1)

    qidx_t = qidx_t_ref[0, 0]
    widx_t = widx_t_ref[0]

    def score_chunk(c, carry):
        kb = kidx_ref[0, pl.ds(pl.multiple_of(c * kc, kc), kc), :]
        logits = _dot(kb, qidx_t)
        score = jnp.zeros((kc, tq), F32)
        for hh in range(C_IDX_HEADS):
            score = score + jnp.maximum(logits[:, hh * tq:(hh + 1) * tq], 0.0) * widx_t[hh:hh + 1, :]
        key_pos = c * kc + lax.broadcasted_iota(I32, (kc, 1), 0)
        key = jnp.where(key_pos <= t_pos, _order_key(score), INT_MIN)
        key_scr[c] = key
        u = key ^ INT_MIN
        for half in range(words_per_chunk):
            slab = [u[half * SLAB_ROWS + 8 * j:half * SLAB_ROWS + 8 * j + 8, :] for j in range(32)]
            planes = _bit_transpose32(slab)
            for b in range(32):
                plane_scr[b, c * words_per_chunk + half] = planes[31 - b]
        return carry

    words_per_chunk = kc // SLAB_ROWS
    n_words = plane_scr.shape[1]
    lax.fori_loop(0, n_chunks, score_chunk, 0)

    for w in range(n_words):
        cand_scr[w] = jnp.full((8, tq), jnp.where(w < n_chunks * words_per_chunk, -1, 0), I32)

    def bit_pass(it, state):
        prefix, need = state
        b = 31 - it
        ones = jnp.zeros((8, tq), I32)
        for w in range(n_words):
            ones = ones + lax.population_count(cand_scr[w] & plane_scr[b, w])
        ones = jnp.sum(ones, axis=0, keepdims=True).astype(F32)
        keep = ones >= need
        for w in range(n_words):
            cand = cand_scr[w]
            hit = cand & plane_scr[b, w]
            cand_scr[w] = jnp.where(keep, hit, cand ^ hit)
        return jnp.where(keep, prefix | (jnp.int32(1) << b), prefix), jnp.where(keep, need, need - ones)

    prefix, need = lax.fori_loop(0, 32, bit_pass, (jnp.zeros((1, tq), I32), jnp.full((1, tq), topk, F32)))
    thr = jnp.maximum(prefix ^ INT_MIN, INT_MIN + 1)
    n_equal = jnp.zeros((8, tq), I32)
    for w in range(n_words):
        n_equal = n_equal + lax.population_count(cand_scr[w])
    n_equal = jnp.sum(n_equal, axis=0, keepdims=True).astype(F32)

    has_surplus = jnp.logical_and(prefix != 0, n_equal > need)

    @pl.when(jnp.max(jnp.where(has_surplus, 1.0, 0.0)) > 0.0)
    def _():
        keep_equal = jnp.where(has_surplus, need, float(kc) * 1e6)
        lower = (lax.broadcasted_iota(I32, (kc, kc), 0) >= lax.broadcasted_iota(I32, (kc, kc), 1))
        lower = jnp.where(lower, 1.0, 0.0).astype(BF16)

        def body(c, seen):
            keys = key_scr[c]
            eq = keys == thr
            rank = seen + _dot(lower, jnp.where(eq, 1.0, 0.0).astype(BF16))
            key_scr[c] = jnp.where(jnp.logical_and(eq, rank > keep_equal), INT_MIN, keys)
            return rank[kc - 1:kc, :]

        lax.fori_loop(0, n_chunks, body, jnp.zeros((1, tq), F32))

    n_groups = s_scr.shape[1]
    m_scr[...] = jnp.full(m_scr.shape, -jnp.inf, F32)
    acc_scr[...] = jnp.zeros(acc_scr.shape, F32)

    def pipeline_step(c_next, c_now):
        if c_next is not None:
            c_next, slot_next = c_next
            kv = ckv_ref[0, pl.ds(pl.multiple_of(c_next * kc, kc), kc), :]
            bias = jnp.where(key_scr[c_next] >= thr, 0.0, -jnp.inf)
            bias = jnp.concatenate([bias] * DSA_GROUP_HEADS, axis=1)
        if c_now is not None:
            c_now, slot_now = c_now
            kv_t = ckv_t_ref[0, c_now]
        for g in range(n_groups):
            if c_next is not None:
                sb = _dot(kv, qlat_t_ref[0, 0, g]) + bias
                s_scr[slot_next, g] = sb
                mx_scr[slot_next, g] = jnp.max(sb, axis=0, keepdims=True)
            if c_now is not None:
                m_old = m_scr[g]
                m_new = jnp.maximum(m_old, mx_scr[slot_now, g])
                m_safe = jnp.where(m_new == -jnp.inf, 0.0, m_new)
                alpha = jnp.exp2(m_old - m_safe)
                p = jnp.exp2(s_scr[slot_now, g] - m_safe)
                m_scr[g] = m_new
                acc_scr[g] = alpha * acc_scr[g] + _dot(kv_t, p.astype(BF16))

    pipeline_step((jnp.int32(0), 0), None)
    n_pairs = (n_chunks - 1) // 2

    def attn_pair(j, carry):
        c = 2 * j
        pipeline_step((c + 1, 1), (c, 0))
        pipeline_step((c + 2, 0), (c + 1, 1))
        return carry

    lax.fori_loop(0, n_pairs, attn_pair, 0)
    last = n_chunks - 1

    @pl.when(last == 2 * n_pairs)
    def _():
        pipeline_step(None, (last, 0))

    @pl.when(last != 2 * n_pairs)
    def _():
        pipeline_step((last, 1), (last - 1, 0))
        pipeline_step(None, (last, 1))
    outs = []
    for hh in range(C_HEADS):
        g, k = divmod(hh, DSA_GROUP_HEADS)
        cols = slice(k * tq, (k + 1) * tq)
        o_lat_t = acc_scr[g, :C_KV_RANK, cols] / acc_scr[g, C_KV_RANK:C_KV_RANK + 1, cols]
        outs.append(_dot_tn(o_lat_t.astype(BF16), wuv_ref[hh]).astype(BF16))
    o_ref[0] = jnp.concatenate(outs, axis=-1)


def _dsa(q_lat_t, q_idx_t, w_idx_t, k_idx, ckv, ckv_t, w_uv):
    bn, s, _ = ckv.shape
    tq = DSA_QUERIES
    kc = ckv_t.shape[-1]
    topk = min(C_MAX_TOPK, s // 4)
    heads_q = C_HEADS * tq
    n_groups, group_q = q_lat_t.shape[2], q_lat_t.shape[4]
    full = lambda n: pl.BlockSpec((1, s, n), lambda b, i: (b, 0, 0))
    width = C_HEADS * C_HEAD_DIM
    return pl.pallas_call(
        functools.partial(_dsa_kernel, topk=float(topk)),
        grid=(bn, s // tq),
        in_specs=[pl.BlockSpec((1, 1, n_groups, C_KV_RANK, group_q), lambda b, i: (b, i, 0, 0, 0)),
                  pl.BlockSpec((1, 1, C_IDX_DIM, heads_q), lambda b, i: (b, i, 0, 0)),
                  pl.BlockSpec((1, C_IDX_HEADS, tq), lambda b, i: (b, 0, i)),
                  full(C_IDX_DIM), full(C_KV_RANK),
                  pl.BlockSpec((1, s // kc, ckv_t.shape[2], kc), lambda b, i: (b, 0, 0, 0)),
                  _const_spec(w_uv.shape)],
        out_specs=pl.BlockSpec((1, tq, width), lambda b, i: (b, i, 0)),
        out_shape=jax.ShapeDtypeStruct((bn, s, width), BF16),
        scratch_shapes=[pltpu.VMEM((s // kc, kc, tq), I32), pltpu.VMEM((32, s // SLAB_ROWS, 8, tq), I32),
                        pltpu.VMEM((s // SLAB_ROWS, 8, tq), I32), pltpu.VMEM((2, n_groups, kc, group_q), F32),
                        pltpu.VMEM((2, n_groups, 1, group_q), F32),
                        pltpu.VMEM((n_groups, 1, group_q), F32),
                        pltpu.VMEM((n_groups, ckv_t.shape[2], group_q), F32)],
        compiler_params=_params(),
        name="dsa",
    )(q_lat_t, q_idx_t, w_idx_t, k_idx, ckv, ckv_t, w_uv.astype(BF16))


def _row_tile(s, want):
    return want if s % want == 0 else s


def kernel(x, mem, hgrn_lower_bounds, even_mix_norm, even_w_in, even_a_out_norm, even_b_conv, even_w_out, odd_mix_norm, odd_w_in, odd_c_q_norm, odd_c_kv_norm, odd_c_w_uq, odd_c_w_uk, odd_c_w_uv, odd_c_idx_wq, odd_c_idx_k_g, odd_c_idx_k_b, odd_d_v_g, odd_d_v_b, odd_d_w_s, odd_d_b_s, odd_w_out, xa_norm, xa_mem_norm, xa_wq, xa_wkv, xa_wo, ffn_norm, ffn_w13, ffn_w2, final_norm):
    s = x.shape[1]
    tm = _row_tile(s, 512)
    k_mem, v_mem = _mem_kv(mem, xa_mem_norm, xa_wkv)

    def xattn_ffn(x, layer, last, **mix):
        return _xattn_ffn(x, k_mem[layer], v_mem[layer], xa_norm[layer], xa_wq[layer], xa_wo[layer],
                          ffn_norm[layer], ffn_w13[layer], ffn_w2[layer], final_norm,
                          final_norm=last, tm=tm, **mix)

    x = _even_mixer(x, hgrn_lower_bounds, even_mix_norm[0], even_w_in[0], even_a_out_norm[0],
                    even_b_conv[0], even_w_out[0], tm=tm)
    x = xattn_ffn(x, 0, False)
    q_lat_t, q_idx_t, w_idx_t, ckv, ckv_t, k_idx, o_d = _odd_proj(
        x, odd_mix_norm[0], odd_w_in[0], odd_c_q_norm[0], odd_c_kv_norm[0], odd_c_w_uq[0], odd_c_w_uk[0],
        odd_c_idx_wq[0], odd_c_idx_k_g[0], odd_c_idx_k_b[0], odd_d_v_g[0], odd_d_v_b[0],
        odd_d_w_s[0], odd_d_b_s[0], tm=tm)
    o_c = _dsa(q_lat_t, q_idx_t, w_idx_t, k_idx, ckv, ckv_t, odd_c_w_uv[0])
    return xattn_ffn(x, 1, True, mix=(o_c, o_d), w_mix=odd_w_out[0])
```

```python
import functools

import jax
import jax.numpy as jnp
import numpy as np
from jax import lax
from jax.experimental import pallas as pl
from jax.experimental.pallas import tpu as pltpu

F32 = jnp.float32
BF16 = jnp.bfloat16
I32 = jnp.int32

V7X_VMEM_BYTES = 64 * 1024 * 1024
VMEM_LIMIT = 56 * 1024 * 1024
LANES = 128

A_HEADS, A_KDIM, A_VDIM, A_CHUNK = 4, 128, 128, 64
A_WIDTH = A_HEADS * A_VDIM
B_WIDTH = 512
C_HEADS, C_HEAD_DIM, C_Q_RANK, C_KV_RANK = 8, 64, 256, 128
C_IDX_HEADS, C_IDX_DIM, C_MAX_TOPK = 8, 64, 256
D_GROUPS, D_CHUNK, D_WIDTH = 4, 128, 512
X_HEADS = 4
HGRN_LEVELS = 6
HGRN_DIRECT_MAX_DECAY = 64.0
DSA_KEY_CHUNK = 512
SLAB_ROWS = 32 * 8
DSA_SUM_ROWS = 16
DSA_QUERIES = 256
DSA_GROUP_HEADS = 1
LOG2E = 1.4426950408889634
INT_MIN = -(2 ** 31)


def _dot(a, b):
    return jnp.dot(a, b, preferred_element_type=F32)


def _dot_nt(a, b):
    return lax.dot_general(a, b, (((1,), (1,)), ((), ())), preferred_element_type=F32)


def _dot_tn(a, b):
    return lax.dot_general(a, b, (((0,), (0,)), ((), ())), preferred_element_type=F32)


def _rms(x, g, eps=1e-6):
    return x * lax.rsqrt(jnp.mean(x * x, axis=-1, keepdims=True) + eps) * g


def _layer_norm(x, g, b, eps=1e-5):
    mu = jnp.mean(x, axis=-1, keepdims=True)
    xc = x - mu
    var = jnp.mean(xc * xc, axis=-1, keepdims=True)
    return xc * lax.rsqrt(var + eps) * g + b


def _silu(x):
    return x * jax.nn.sigmoid(x)


def _gelu(x):
    return 0.5 * x * (1.0 + lax.erf(x * (2.0 ** -0.5)))


def _const_spec(shape):
    zeros = (0,) * len(shape)
    return pl.BlockSpec(shape, lambda *_: zeros, pipeline_mode=pl.Buffered(1))


def _params(n_parallel=0, n_arbitrary=2):
    return pltpu.CompilerParams(
        dimension_semantics=("parallel",) * n_parallel + ("arbitrary",) * n_arbitrary,
        vmem_limit_bytes=VMEM_LIMIT)


def _mem_kv_kernel(mem_ref, g_ref, wkv_ref, k_ref, v_ref):
    d = mem_ref.shape[-1]
    m = _rms(mem_ref[0], g_ref[0]).astype(BF16)
    kv = _dot(m, wkv_ref[0])
    k_ref[0, 0] = kv[:, :d].astype(BF16)
    v_ref[0, 0] = kv[:, d:].astype(BF16)


def _mem_kv(mem, mem_g, wkv):
    depth, d = mem_g.shape
    bn, m_len, _ = mem.shape
    out = jax.ShapeDtypeStruct((depth, bn, m_len, d), BF16)
    return pl.pallas_call(
        _mem_kv_kernel,
        grid=(depth, bn),
        in_specs=[pl.BlockSpec((1, m_len, d), lambda l, b: (b, 0, 0)),
                  pl.BlockSpec((1, 1, d), lambda l, b: (l, 0, 0)),
                  pl.BlockSpec((1, d, 2 * d), lambda l, b: (l, 0, 0))],
        out_specs=[pl.BlockSpec((1, 1, m_len, d), lambda l, b: (l, b, 0, 0))] * 2,
        out_shape=[out, out],
        compiler_params=_params(),
        name="mem_kv",
    )(mem, mem_g.reshape(depth, 1, d), wkv.astype(BF16))


def _xattn_ffn_kernel(*refs, final_norm, ff_chunk, mix_parts):
    n_mix = mix_parts + 1 if mix_parts else 0
    mix_refs, refs = refs[:n_mix], refs[n_mix:]
    x_ref, k_ref, v_ref, gxa_ref, wq_ref, wo_ref, gff_ref, w13_ref, w2_ref, gfin_ref, o_ref = refs
    x = x_ref[0]
    if mix_parts:
        mixed = jnp.concatenate([r[0] for r in mix_refs[:mix_parts]], axis=-1)
        x = x + _dot(mixed, mix_refs[mix_parts][...])
    d = x.shape[-1]
    hd = d // X_HEADS
    ff = w2_ref.shape[0]
    h = _rms(x, gxa_ref[...]).astype(BF16)
    q = _dot(h, wq_ref[...])
    heads = []
    for i in range(X_HEADS):
        cols = slice(i * hd, (i + 1) * hd)
        sc = _dot_nt(q[:, cols].astype(BF16), k_ref[0, :, cols]) * (hd ** -0.5)
        p = jnp.exp(sc - jnp.max(sc, axis=-1, keepdims=True))
        o = _dot(p.astype(BF16), v_ref[0, :, cols])
        heads.append((o / jnp.sum(p, axis=-1, keepdims=True)).astype(BF16))
    x = x + _dot(jnp.concatenate(heads, axis=-1), wo_ref[...])
    h2 = _rms(x, gff_ref[...]).astype(BF16)
    acc = jnp.zeros_like(x)
    for c0 in range(0, ff, ff_chunk):
        g = _dot(h2, w13_ref[:, c0:c0 + ff_chunk])
        u = _dot(h2, w13_ref[:, ff + c0:ff + c0 + ff_chunk])
        acc = acc + _dot((_silu(g) * u).astype(BF16), w2_ref[c0:c0 + ff_chunk, :])
    x = x + acc
    if final_norm:
        x = _rms(x, gfin_ref[...])
    o_ref[0] = x


def _xattn_ffn(x, k_mem, v_mem, g_xa, wq, wo, g_ff, w13, w2, g_fin, *, final_norm, tm, mix=(), w_mix=None):
    bn, s, d = x.shape
    mix_specs = [pl.BlockSpec((1, tm, a.shape[-1]), lambda b, i: (b, i, 0)) for a in mix]
    mix_args = list(mix)
    if mix:
        mix_specs.append(_const_spec(w_mix.shape))
        mix_args.append(w_mix.astype(BF16))
    m_len = k_mem.shape[1]
    ff = w2.shape[0]
    ff_chunk = ff // 2 if (ff // 2) % LANES == 0 else ff
    row = lambda a: a.reshape(1, d)
    kern = functools.partial(_xattn_ffn_kernel, final_norm=final_norm, ff_chunk=ff_chunk, mix_parts=len(mix))
    return pl.pallas_call(
        kern,
        grid=(bn, s // tm),
        in_specs=mix_specs + [
                  pl.BlockSpec((1, tm, d), lambda b, i: (b, i, 0)),
                  pl.BlockSpec((1, m_len, d), lambda b, i: (b, 0, 0)),
                  pl.BlockSpec((1, m_len, d), lambda b, i: (b, 0, 0)),
                  _const_spec((1, d)), _const_spec((d, d)), _const_spec((d, d)),
                  _const_spec((1, d)), _const_spec((d, 2 * ff)), _const_spec((ff, d)),
                  _const_spec((1, d))],
        out_specs=pl.BlockSpec((1, tm, d), lambda b, i: (b, i, 0)),
        out_shape=jax.ShapeDtypeStruct(x.shape, F32),
        compiler_params=_params(),
        name="xattn_ffn_final" if final_norm else "xattn_ffn",
    )(*mix_args, x, k_mem, v_mem, row(g_xa), wq.astype(BF16), wo.astype(BF16), row(g_ff),
      w13.astype(BF16), w2.astype(BF16), row(g_fin))


def _hgrn_tables():
    n = A_CHUNK
    t = np.arange(n)[:, None]
    u = np.arange(n)[None, :]
    sums, pair = [], []
    for lvl in range(HGRN_LEVELS):
        c = n >> (lvl + 1)
        mid = (t // (2 * c)) * (2 * c) + c
        late = t >= mid
        sums.append(np.where(late, (u >= mid) & (u <= t), (u > t) & (u < mid)))
        pair.append(late & (u < mid) & (u >= mid - c))
    sums.append(u <= t)
    sums.append(u > t)
    pair.append(t == u)
    return (np.concatenate(sums, axis=0).astype(np.float32),
            np.stack(pair, axis=0).astype(np.float32))


def _even_kernel(x_ref, lbp_ref, g_ref, win_ref, aon_ref, conv_ref, wout_ref, sums_ref, pair_ref,
                 o_ref, q_scr, k_scr, v_scr, lfh_scr, lfl_scr, oa_scr, z_scr, st_scr):
    tm = x_ref.shape[1]
    n = A_CHUNK

    @pl.when(pl.program_id(1) == 0)
    def _():
        st_scr[...] = jnp.zeros_like(st_scr)
        z_scr[0:8, :] = jnp.zeros((8, B_WIDTH), F32)

    x = x_ref[0]
    h = _rms(x, g_ref[...]).astype(BF16)

    def proj(j, width=A_WIDTH):
        return _dot(h, win_ref[:, j * width:(j + 1) * width])

    lbp = lbp_ref[...]
    e = jnp.exp(lbp - jnp.max(lbp, axis=0, keepdims=True))
    lb = e[0:1, :] / jnp.sum(e, axis=0, keepdims=True)

    q_scr[...] = _silu(proj(0))
    f = lb + (1.0 - lb) * jax.nn.sigmoid(proj(1))
    k_scr[...] = 1.0 - f
    logf = jnp.log(f)
    lf_hi = logf.astype(BF16)
    lfh_scr[...] = lf_hi
    lfl_scr[...] = (logf - lf_hi.astype(F32)).astype(BF16)
    v_scr[...] = proj(2)

    row = lax.broadcasted_iota(I32, (n, 1), 0)
    causal = jnp.where(row >= lax.broadcasted_iota(I32, (1, n), 1), 1.0, 0.0)

    def chunk(c, carry, direct):
        r0 = pl.multiple_of(c * n, n)
        rows = pl.ds(r0, n)
        lf_hi, lf_lo = lfh_scr[rows, :], lfl_scr[rows, :]
        qc, kc, vc = q_scr[rows, :], k_scr[rows, :], v_scr[rows, :]
        edge = sums_ref[HGRN_LEVELS * n:, :]
        d_edge = _dot(edge, lf_hi) + _dot(edge, lf_lo)
        b = d_edge[:n]
        e_in, e_out = jnp.exp(b), jnp.exp(d_edge[n:])
        if direct:
            e_neg = jnp.exp(-b)
        else:
            tree = sums_ref[:HGRN_LEVELS * n, :]
            ex = jnp.exp(_dot(tree, lf_hi) + _dot(tree, lf_lo))
        for hh in range(A_HEADS):
            cols = slice(hh * A_KDIM, (hh + 1) * A_KDIM)
            qh, kh = qc[:, cols], kc[:, cols]
            qin = (qh * e_in[:, cols]).astype(BF16)
            if direct:
                attn = _dot_nt(qin, (kh * e_neg[:, cols]).astype(BF16)) * causal
            else:
                attn = _dot_nt(qh.astype(BF16), kh.astype(BF16)) * pair_ref[HGRN_LEVELS]
                for lvl in range(HGRN_LEVELS):
                    late = (row & (n >> (lvl + 1))) != 0
                    xl = (jnp.where(late, qh, kh) * ex[lvl * n:(lvl + 1) * n, cols]).astype(BF16)
                    attn = attn + _dot_nt(xl, xl) * pair_ref[lvl]
            st = st_scr[hh]
            vh = vc[:, cols].astype(BF16)
            oa_scr[rows, cols] = _dot(attn.astype(BF16), vh) + _dot_nt(qin, st.astype(BF16))
            st_scr[hh] = st * e_in[n - 1:n, cols] + _dot_tn(vh, (kh * e_out[:, cols]).astype(BF16))
        return carry

    decay = -jnp.sum(logf.reshape(tm // n, n, A_WIDTH), axis=1)
    mild = jnp.max(decay) <= HGRN_DIRECT_MAX_DECAY

    @pl.when(mild)
    def _():
        lax.fori_loop(0, tm // n, functools.partial(chunk, direct=True), 0, unroll=2)

    @pl.when(jnp.logical_not(mild))
    def _():
        lax.fori_loop(0, tm // n, functools.partial(chunk, direct=False), 0, unroll=2)

    gate = _silu(proj(3))
    parts = []
    for hh in range(A_HEADS):
        cols = slice(hh * A_VDIM, (hh + 1) * A_VDIM)
        parts.append((_rms(oa_scr[:, cols], aon_ref[hh:hh + 1, :]) * gate[:, cols]).astype(BF16))

    z_scr[8:8 + tm, :] = proj(5) * proj(6)
    w = conv_ref[...]
    y = (w[0:1, :] * z_scr[6:6 + tm, :] + w[1:2, :] * z_scr[7:7 + tm, :] + w[2:3, :] * z_scr[8:8 + tm, :])
    z_scr[0:8, :] = z_scr[tm:tm + 8, :]
    parts.append((proj(4) * y).astype(BF16))

    o_ref[0] = x + _dot(jnp.concatenate(parts, axis=-1), wout_ref[...])


def _even_mixer(x, lbp, g, w_in, a_out_norm, b_conv, w_out, *, tm):
    bn, s, d = x.shape
    sums, pair = _hgrn_tables()
    n_in = w_in.shape[1]
    return pl.pallas_call(
        _even_kernel,
        grid=(bn, s // tm),
        in_specs=[pl.BlockSpec((1, tm, d), lambda b, i: (b, i, 0)),
                  _const_spec(lbp.shape), _const_spec((1, d)), _const_spec((d, n_in)),
                  _const_spec(a_out_norm.shape), _const_spec(b_conv.shape),
                  _const_spec(w_out.shape), _const_spec(sums.shape), _const_spec(pair.shape)],
        out_specs=pl.BlockSpec((1, tm, d), lambda b, i: (b, i, 0)),
        out_shape=jax.ShapeDtypeStruct(x.shape, F32),
        scratch_shapes=[pltpu.VMEM((tm, A_WIDTH), F32), pltpu.VMEM((tm, A_WIDTH), F32),
                        pltpu.VMEM((tm, A_WIDTH), F32), pltpu.VMEM((tm, A_WIDTH), BF16),
                        pltpu.VMEM((tm, A_WIDTH), BF16), pltpu.VMEM((tm, A_WIDTH), F32),
                        pltpu.VMEM((tm + 8, B_WIDTH), F32),
                        pltpu.VMEM((A_HEADS, A_VDIM, A_KDIM), F32)],
        compiler_params=_params(),
        name="even_mixer",
    )(x, lbp, g.reshape(1, d), w_in.astype(BF16), a_out_norm, b_conv, w_out.astype(BF16),
      jnp.asarray(sums, BF16), jnp.asarray(pair, F32))


def _odd_proj_kernel(x_ref, g_ref, win_ref, wwt_ref, qg_ref, kvg_ref, wuq_ref, wukt_ref, iwqt_ref, ikg_ref, ikb_ref,
                     vg_ref, vb_ref, ws_ref, bs_ref,
                     qlat_t_ref, qidx_t_ref, widx_t_ref, ckv_ref, ckv_t_ref, kidx_ref, od_ref):
    tm = x_ref.shape[1]
    tq = DSA_QUERIES
    kc = ckv_t_ref.shape[-1]
    h = _rms(x_ref[0], g_ref[...]).astype(BF16)
    o = 0

    def proj(width):
        nonlocal o
        r = _dot(h, win_ref[:, o:o + width])
        o += width
        return r

    cq = _rms(proj(C_Q_RANK), qg_ref[...]).astype(BF16)
    ckv = _rms(proj(C_KV_RANK), kvg_ref[...])
    ckv_ref[0] = ckv.astype(BF16)
    for j in range(tm // kc):
        ckv_t_ref[0, j, :C_KV_RANK, :] = ckv[j * kc:(j + 1) * kc, :].T.astype(BF16)
        ckv_t_ref[0, j, C_KV_RANK:, :] = jnp.ones((DSA_SUM_ROWS, kc), BF16)
    du = proj(D_WIDTH)
    dv = proj(D_WIDTH)
    kidx_ref[0] = _layer_norm(proj(LANES)[:, :C_IDX_DIM], ikg_ref[...], ikb_ref[...]).astype(BF16)
    widx_t_ref[0] = _dot_nt(wwt_ref[...], h)[:C_IDX_HEADS, :] * (C_IDX_HEADS ** -0.5)

    q = _dot(cq, wuq_ref[...])
    for i in range(C_HEADS):
        qh = q[:, i * C_HEAD_DIM:(i + 1) * C_HEAD_DIM].astype(BF16)
        lat_t = (_dot_nt(wukt_ref[i], qh) * (C_HEAD_DIM ** -0.5 * LOG2E)).astype(BF16)
        idx_t = (_dot_nt(iwqt_ref[i], cq) * (C_IDX_DIM ** -0.5)).astype(BF16)
        g, k = divmod(i, DSA_GROUP_HEADS)
        for j in range(tm // tq):
            qlat_t_ref[0, j, g, :, k * tq:(k + 1) * tq] = lat_t[:, j * tq:(j + 1) * tq]
            qidx_t_ref[0, j, :, i * tq:(i + 1) * tq] = idx_t[:, j * tq:(j + 1) * tq]

    u = _gelu(du)
    v = _layer_norm(_gelu(dv), vg_ref[...], vb_ref[...]).astype(BF16)
    tri = (lax.broadcasted_iota(I32, (D_CHUNK, D_CHUNK), 0) >= lax.broadcasted_iota(I32, (D_CHUNK, D_CHUNK), 1))
    gw = D_WIDTH // D_GROUPS
    for gi in range(D_GROUPS):
        wc = jnp.where(tri, ws_ref[gi], 0.0).astype(BF16)
        cols = slice(gi * gw, (gi + 1) * gw)
        for c0 in range(0, tm, D_CHUNK):
            mixed = _dot(wc, v[c0:c0 + D_CHUNK, cols]) + bs_ref[:, gi:gi + 1]
            od_ref[0, c0:c0 + D_CHUNK, cols] = (u[c0:c0 + D_CHUNK, cols] * mixed).astype(BF16)


def _odd_proj(x, g, w_in, q_g, kv_g, w_uq, w_uk, idx_wq, ik_g, ik_b, v_g, v_b, w_s, b_s, *, tm):
    bn, s, d = x.shape
    c0, c1, c2, c3 = C_Q_RANK, C_Q_RANK + C_KV_RANK, C_Q_RANK + C_KV_RANK + C_IDX_DIM, \
        C_Q_RANK + C_KV_RANK + C_IDX_DIM + C_IDX_HEADS
    pad = LANES - C_IDX_DIM
    w = jnp.concatenate([w_in[:, :c1], w_in[:, c3:], w_in[:, c1:c2], jnp.zeros((d, pad), w_in.dtype)],
                        axis=1).astype(BF16)
    ww_t = jnp.concatenate([w_in[:, c2:c3].T, jnp.zeros((16 - C_IDX_HEADS, d), w_in.dtype)], axis=0).astype(BF16)
    wuk_t = jnp.transpose(w_uk, (0, 2, 1)).astype(BF16)
    iwq_t = jnp.transpose(idx_wq.reshape(C_Q_RANK, C_IDX_HEADS, C_IDX_DIM), (1, 2, 0)).astype(BF16)
    kc = min(DSA_KEY_CHUNK, s)
    assert tm % kc == 0 and tm % DSA_QUERIES == 0
    row = lambda a: a.reshape(1, -1)
    shp = lambda n, dt: jax.ShapeDtypeStruct((bn, s, n), dt)
    blk = lambda n: pl.BlockSpec((1, tm, n), lambda b, i: (b, i, 0))
    nq = s // DSA_QUERIES
    heads_q = C_HEADS * DSA_QUERIES
    n_groups, group_q = C_HEADS // DSA_GROUP_HEADS, DSA_GROUP_HEADS * DSA_QUERIES
    qblk = lambda n: pl.BlockSpec((1, tm // DSA_QUERIES, n, heads_q), lambda b, i: (b, i, 0, 0))
    return pl.pallas_call(
        _odd_proj_kernel,
        grid=(bn, s // tm),
        in_specs=[blk(d), _const_spec((1, d)), _const_spec(w.shape), _const_spec(ww_t.shape),
                  _const_spec((1, C_Q_RANK)), _const_spec((1, C_KV_RANK)),
                  _const_spec(w_uq.shape), _const_spec(wuk_t.shape), _const_spec(iwq_t.shape),
                  _const_spec((1, C_IDX_DIM)), _const_spec((1, C_IDX_DIM)),
                  _const_spec((1, D_WIDTH)), _const_spec((1, D_WIDTH)),
                  _const_spec(w_s.shape), _const_spec((D_CHUNK, D_GROUPS))],
        out_specs=[pl.BlockSpec((1, tm // DSA_QUERIES, n_groups, C_KV_RANK, group_q), lambda b, i: (b, i, 0, 0, 0)),
                   qblk(C_IDX_DIM),
                   pl.BlockSpec((1, C_IDX_HEADS, tm), lambda b, i: (b, 0, i)),
                   blk(C_KV_RANK),
                   pl.BlockSpec((1, tm // kc, C_KV_RANK + DSA_SUM_ROWS, kc), lambda b, i: (b, i, 0, 0)),
                   blk(C_IDX_DIM), blk(D_WIDTH)],
        out_shape=[jax.ShapeDtypeStruct((bn, nq, n_groups, C_KV_RANK, group_q), BF16),
                   jax.ShapeDtypeStruct((bn, nq, C_IDX_DIM, heads_q), BF16),
                   jax.ShapeDtypeStruct((bn, C_IDX_HEADS, s), F32),
                   shp(C_KV_RANK, BF16),
                   jax.ShapeDtypeStruct((bn, s // kc, C_KV_RANK + DSA_SUM_ROWS, kc), BF16),
                   shp(C_IDX_DIM, BF16), shp(D_WIDTH, BF16)],
        compiler_params=_params(),
        name="odd_proj",
    )(x, row(g), w, ww_t, row(q_g), row(kv_g), w_uq.astype(BF16), wuk_t, iwq_t,
      row(ik_g), row(ik_b), row(v_g), row(v_b), w_s, b_s.T)


def _bit_transpose32(words):
    a = list(words)
    j, m = 16, 0x0000FFFF
    while j:
        mask = np.int32(np.uint32(m))
        k = 0
        while k < 32:
            t = (a[k] ^ lax.shift_right_logical(a[k + j], j)) & mask
            a[k] = a[k] ^ t
            a[k + j] = a[k + j] ^ (t << j)
            k = (k + j + 1) & ~j
        j >>= 1
        m = (m ^ (m << j)) & 0xFFFFFFFF
    return a


def _order_key(score):
    bits = lax.bitcast_convert_type(score + 0.0, I32)
    return bits ^ ((bits >> 31) & 0x7FFFFFFF)


def _dsa_kernel(qlat_t_ref, qidx_t_ref, widx_t_ref, kidx_ref, ckv_ref, ckv_t_ref, wuv_ref,
                o_ref, key_scr, lg_scr, plane_scr, cand_scr, s_scr, mx_scr, m_scr, acc_scr, *, topk):
    tq = DSA_QUERIES
    kc = key_scr.shape[1]
    i = pl.program_id(1)
    n_chunks = ((i + 1) * tq + kc - 1) // kc
    t_pos = i * tq + lax.broadcasted_iota(I32, (1, tq), 1)

    qidx_t = qidx_t_ref[0, 0]
    widx_t = widx_t_ref[0]

    def score_stage(c_next, c_now):
        if c_next is not None:
            c_next, slot_next = c_next
            kb = kidx_ref[0, pl.ds(pl.multiple_of(c_next * kc, kc), kc), :]
            lg_scr[slot_next] = _dot(kb, qidx_t)
        if c_now is None:
            return
        c, slot = c_now
        score = jnp.zeros((kc, tq), F32)
        for hh in range(C_IDX_HEADS):
            score = score + jnp.maximum(lg_scr[slot, :, hh * tq:(hh + 1) * tq], 0.0) * widx_t[hh:hh + 1, :]
        key_pos = c * kc + lax.broadcasted_iota(I32, (kc, 1), 0)
        key = jnp.where(key_pos <= t_pos, _order_key(score), INT_MIN)
        key_scr[c] = key
        u = key ^ INT_MIN
        for half in range(words_per_chunk):
            slab = [u[half * SLAB_ROWS + 8 * j:half * SLAB_ROWS + 8 * j + 8, :] for j in range(32)]
            planes = _bit_transpose32(slab)
            for b in range(32):
                plane_scr[b, c * words_per_chunk + half] = planes[31 - b]

    def score_chunks():
        score_stage((jnp.int32(0), 0), None)
        pairs = (n_chunks - 1) // 2

        def pair(j, carry):
            c = 2 * j
            score_stage((c + 1, 1), (c, 0))
            score_stage((c + 2, 0), (c + 1, 1))
            return carry

        lax.fori_loop(0, pairs, pair, 0)
        last = n_chunks - 1

        @pl.when(last == 2 * pairs)
        def _():
            score_stage(None, (last, 0))

        @pl.when(last != 2 * pairs)
        def _():
            score_stage((last, 1), (last - 1, 0))
            score_stage(None, (last, 1))

    words_per_chunk = kc // SLAB_ROWS
    n_words = plane_scr.shape[1]

    @pl.when(i == 0)
    def _():
        plane_scr[...] = jnp.zeros(plane_scr.shape, I32)

    score_chunks()

    for w in range(n_words):
        cand_scr[w] = jnp.full((8, tq), jnp.where(w < n_chunks * words_per_chunk, -1, 0), I32)

    def bit_pass(it, state):
        prefix, need = state
        b = 31 - it
        ones = jnp.zeros((8, tq), I32)
        for w in range(n_words):
            ones = ones + lax.population_count(cand_scr[w] & plane_scr[b, w])
        ones = jnp.sum(ones, axis=0, keepdims=True).astype(F32)
        keep = ones >= need
        for w in range(n_words):
            cand = cand_scr[w]
            hit = cand & plane_scr[b, w]
            cand_scr[w] = jnp.where(keep, hit, cand ^ hit)
        return jnp.where(keep, prefix | (jnp.int32(1) << b), prefix), jnp.where(keep, need, need - ones)

    prefix, need = lax.fori_loop(0, 32, bit_pass, (jnp.zeros((1, tq), I32), jnp.full((1, tq), topk, F32)))
    thr = jnp.maximum(prefix ^ INT_MIN, INT_MIN + 1)
    n_equal = jnp.zeros((8, tq), I32)
    for w in range(n_words):
        n_equal = n_equal + lax.population_count(cand_scr[w])
    n_equal = jnp.sum(n_equal, axis=0, keepdims=True).astype(F32)

    has_surplus = jnp.logical_and(prefix != 0, n_equal > need)

    @pl.when(jnp.max(jnp.where(has_surplus, 1.0, 0.0)) > 0.0)
    def _():
        keep_equal = jnp.where(has_surplus, need, float(kc) * 1e6)
        lower = (lax.broadcasted_iota(I32, (kc, kc), 0) >= lax.broadcasted_iota(I32, (kc, kc), 1))
        lower = jnp.where(lower, 1.0, 0.0).astype(BF16)

        def body(c, seen):
            keys = key_scr[c]
            eq = keys == thr
            rank = seen + _dot(lower, jnp.where(eq, 1.0, 0.0).astype(BF16))
            key_scr[c] = jnp.where(jnp.logical_and(eq, rank > keep_equal), INT_MIN, keys)
            return rank[kc - 1:kc, :]

        lax.fori_loop(0, n_chunks, body, jnp.zeros((1, tq), F32))

    n_groups = s_scr.shape[1]
    m_scr[...] = jnp.full(m_scr.shape, -jnp.inf, F32)
    acc_scr[...] = jnp.zeros(acc_scr.shape, F32)

    def pipeline_step(c_next, c_now):
        if c_next is not None:
            c_next, slot_next = c_next
            kv = ckv_ref[0, pl.ds(pl.multiple_of(c_next * kc, kc), kc), :]
            bias = jnp.where(key_scr[c_next] >= thr, 0.0, -jnp.inf)
            bias = jnp.concatenate([bias] * DSA_GROUP_HEADS, axis=1)
        if c_now is not None:
            c_now, slot_now = c_now
            kv_t = ckv_t_ref[0, c_now]
        for g in range(n_groups):
            if c_next is not None:
                sb = _dot(kv, qlat_t_ref[0, 0, g]) + bias
                s_scr[slot_next, g] = sb
                mx_scr[slot_next, g] = jnp.max(sb, axis=0, keepdims=True)
            if c_now is not None:
                m_old = m_scr[g]
                m_new = jnp.maximum(m_old, mx_scr[slot_now, g])
                m_safe = jnp.where(m_new == -jnp.inf, 0.0, m_new)
                alpha = jnp.exp2(m_old - m_safe)
                p = jnp.exp2(s_scr[slot_now, g] - m_safe)
                m_scr[g] = m_new
                acc_scr[g] = alpha * acc_scr[g] + _dot(kv_t, p.astype(BF16))

    pipeline_step((jnp.int32(0), 0), None)
    n_pairs = (n_chunks - 1) // 2

    def attn_pair(j, carry):
        c = 2 * j
        pipeline_step((c + 1, 1), (c, 0))
        pipeline_step((c + 2, 0), (c + 1, 1))
        return carry

    lax.fori_loop(0, n_pairs, attn_pair, 0)
    last = n_chunks - 1

    @pl.when(last == 2 * n_pairs)
    def _():
        pipeline_step(None, (last, 0))

    @pl.when(last != 2 * n_pairs)
    def _():
        pipeline_step((last, 1), (last - 1, 0))
        pipeline_step(None, (last, 1))
    outs = []
    for hh in range(C_HEADS):
        g, k = divmod(hh, DSA_GROUP_HEADS)
        cols = slice(k * tq, (k + 1) * tq)
        o_lat_t = acc_scr[g, :C_KV_RANK, cols] / acc_scr[g, C_KV_RANK:C_KV_RANK + 1, cols]
        outs.append(_dot_tn(o_lat_t.astype(BF16), wuv_ref[hh]).astype(BF16))
    o_ref[0] = jnp.concatenate(outs, axis=-1)


def _dsa(q_lat_t, q_idx_t, w_idx_t, k_idx, ckv, ckv_t, w_uv):
    bn, s, _ = ckv.shape
    tq = DSA_QUERIES
    kc = ckv_t.shape[-1]
    topk = min(C_MAX_TOPK, s // 4)
    heads_q = C_HEADS * tq
    n_groups, group_q = q_lat_t.shape[2], q_lat_t.shape[4]
    full = lambda n: pl.BlockSpec((1, s, n), lambda b, i: (b, 0, 0))
    width = C_HEADS * C_HEAD_DIM
    return pl.pallas_call(
        functools.partial(_dsa_kernel, topk=float(topk)),
        grid=(bn, s // tq),
        in_specs=[pl.BlockSpec((1, 1, n_groups, C_KV_RANK, group_q), lambda b, i: (b, i, 0, 0, 0)),
                  pl.BlockSpec((1, 1, C_IDX_DIM, heads_q), lambda b, i: (b, i, 0, 0)),
                  pl.BlockSpec((1, C_IDX_HEADS, tq), lambda b, i: (b, 0, i)),
                  full(C_IDX_DIM), full(C_KV_RANK),
                  pl.BlockSpec((1, s // kc, ckv_t.shape[2], kc), lambda b, i: (b, 0, 0, 0)),
                  _const_spec(w_uv.shape)],
        out_specs=pl.BlockSpec((1, tq, width), lambda b, i: (b, i, 0)),
        out_shape=jax.ShapeDtypeStruct((bn, s, width), BF16),
        scratch_shapes=[pltpu.VMEM((s // kc, kc, tq), I32), pltpu.VMEM((2, kc, heads_q), F32),
                        pltpu.VMEM((32, s // SLAB_ROWS, 8, tq), I32),
                        pltpu.VMEM((s // SLAB_ROWS, 8, tq), I32), pltpu.VMEM((2, n_groups, kc, group_q), F32),
                        pltpu.VMEM((2, n_groups, 1, group_q), F32),
                        pltpu.VMEM((n_groups, 1, group_q), F32),
                        pltpu.VMEM((n_groups, ckv_t.shape[2], group_q), F32)],
        compiler_params=_params(),
        name="dsa",
    )(q_lat_t, q_idx_t, w_idx_t, k_idx, ckv, ckv_t, w_uv.astype(BF16))


def _row_tile(s, want):
    return want if s % want == 0 else s


def kernel(x, mem, hgrn_lower_bounds, even_mix_norm, even_w_in, even_a_out_norm, even_b_conv, even_w_out, odd_mix_norm, odd_w_in, odd_c_q_norm, odd_c_kv_norm, odd_c_w_uq, odd_c_w_uk, odd_c_w_uv, odd_c_idx_wq, odd_c_idx_k_g, odd_c_idx_k_b, odd_d_v_g, odd_d_v_b, odd_d_w_s, odd_d_b_s, odd_w_out, xa_norm, xa_mem_norm, xa_wq, xa_wkv, xa_wo, ffn_norm, ffn_w13, ffn_w2, final_norm):
    s = x.shape[1]
    tm = _row_tile(s, 512)
    k_mem, v_mem = _mem_kv(mem, xa_mem_norm, xa_wkv)

    def xattn_ffn(x, layer, last, **mix):
        return _xattn_ffn(x, k_mem[layer], v_mem[layer], xa_norm[layer], xa_wq[layer], xa_wo[layer],
                          ffn_norm[layer], ffn_w13[layer], ffn_w2[layer], final_norm,
                          final_norm=last, tm=tm, **mix)

    x = _even_mixer(x, hgrn_lower_bounds, even_mix_norm[0], even_w_in[0], even_a_out_norm[0],
                    even_b_conv[0], even_w_out[0], tm=tm)
    x = xattn_ffn(x, 0, False)
    q_lat_t, q_idx_t, w_idx_t, ckv, ckv_t, k_idx, o_d = _odd_proj(
        x, odd_mix_norm[0], odd_w_in[0], odd_c_q_norm[0], odd_c_kv_norm[0], odd_c_w_uq[0], odd_c_w_uk[0],
        odd_c_idx_wq[0], odd_c_idx_k_g[0], odd_c_idx_k_b[0], odd_d_v_g[0], odd_d_v_b[0],
        odd_d_w_s[0], odd_d_b_s[0], tm=tm)
    o_c = _dsa(q_lat_t, q_idx_t, w_idx_t, k_idx, ckv, ckv_t, odd_c_w_uv[0])
    return xattn_ffn(x, 1, True, mix=(o_c, o_d), w_mix=odd_w_out[0])
```

```python
import functools

import jax
import jax.numpy as jnp
import numpy as np
from jax import lax
from jax.experimental import pallas as pl
from jax.experimental.pallas import tpu as pltpu

F32 = jnp.float32
BF16 = jnp.bfloat16
I32 = jnp.int32

V7X_VMEM_BYTES = 64 * 1024 * 1024
VMEM_LIMIT = 56 * 1024 * 1024
LANES = 128

A_HEADS, A_KDIM, A_VDIM, A_CHUNK = 4, 128, 128, 64
A_WIDTH = A_HEADS * A_VDIM
B_WIDTH = 512
C_HEADS, C_HEAD_DIM, C_Q_RANK, C_KV_RANK = 8, 64, 256, 128
C_IDX_HEADS, C_IDX_DIM, C_MAX_TOPK = 8, 64, 256
D_GROUPS, D_CHUNK, D_WIDTH = 4, 128, 512
X_HEADS = 4
HGRN_LEVELS = 6
HGRN_DIRECT_MAX_DECAY = 64.0
DSA_KEY_CHUNK = 512
SLAB_ROWS = 32 * 8
DSA_SUM_ROWS = 16
DSA_QUERIES = 256
DSA_GROUP_HEADS = 1
LOG2E = 1.4426950408889634
INT_MIN = -(2 ** 31)


def _dot(a, b):
    return jnp.dot(a, b, preferred_element_type=F32)


def _dot_nt(a, b):
    return lax.dot_general(a, b, (((1,), (1,)), ((), ())), preferred_element_type=F32)


def _dot_tn(a, b):
    return lax.dot_general(a, b, (((0,), (0,)), ((), ())), preferred_element_type=F32)


def _rms(x, g, eps=1e-6):
    return x * lax.rsqrt(jnp.mean(x * x, axis=-1, keepdims=True) + eps) * g


def _layer_norm(x, g, b, eps=1e-5):
    mu = jnp.mean(x, axis=-1, keepdims=True)
    xc = x - mu
    var = jnp.mean(xc * xc, axis=-1, keepdims=True)
    return xc * lax.rsqrt(var + eps) * g + b


def _silu(x):
    return x * jax.nn.sigmoid(x)


def _gelu(x):
    return 0.5 * x * (1.0 + lax.erf(x * (2.0 ** -0.5)))


def _const_spec(shape):
    zeros = (0,) * len(shape)
    return pl.BlockSpec(shape, lambda *_: zeros, pipeline_mode=pl.Buffered(1))


def _params(n_parallel=0, n_arbitrary=2):
    return pltpu.CompilerParams(
        dimension_semantics=("parallel",) * n_parallel + ("arbitrary",) * n_arbitrary,
        vmem_limit_bytes=VMEM_LIMIT)


def _mem_kv_kernel(mem_ref, g_ref, wkv_ref, k_ref, v_ref):
    d = mem_ref.shape[-1]
    m = _rms(mem_ref[0], g_ref[0]).astype(BF16)
    kv = _dot(m, wkv_ref[0])
    k_ref[0, 0] = kv[:, :d].astype(BF16)
    v_ref[0, 0] = kv[:, d:].astype(BF16)


def _mem_kv(mem, mem_g, wkv):
    depth, d = mem_g.shape
    bn, m_len, _ = mem.shape
    out = jax.ShapeDtypeStruct((depth, bn, m_len, d), BF16)
    return pl.pallas_call(
        _mem_kv_kernel,
        grid=(depth, bn),
        in_specs=[pl.BlockSpec((1, m_len, d), lambda l, b: (b, 0, 0)),
                  pl.BlockSpec((1, 1, d), lambda l, b: (l, 0, 0)),
                  pl.BlockSpec((1, d, 2 * d), lambda l, b: (l, 0, 0))],
        out_specs=[pl.BlockSpec((1, 1, m_len, d), lambda l, b: (l, b, 0, 0))] * 2,
        out_shape=[out, out],
        compiler_params=_params(),
        name="mem_kv",
    )(mem, mem_g.reshape(depth, 1, d), wkv.astype(BF16))


def _xattn_ffn_kernel(*refs, final_norm, ff_chunk, mix_parts):
    n_mix = mix_parts + 1 if mix_parts else 0
    mix_refs, refs = refs[:n_mix], refs[n_mix:]
    x_ref, k_ref, v_ref, gxa_ref, wq_ref, wo_ref, gff_ref, w13_ref, w2_ref, gfin_ref, o_ref = refs
    x = x_ref[0]
    if mix_parts:
        mixed = jnp.concatenate([r[0] for r in mix_refs[:mix_parts]], axis=-1)
        x = x + _dot(mixed, mix_refs[mix_parts][...])
    d = x.shape[-1]
    hd = d // X_HEADS
    ff = w2_ref.shape[0]
    h = _rms(x, gxa_ref[...]).astype(BF16)
    q = _dot(h, wq_ref[...])
    heads = []
    for i in range(X_HEADS):
        cols = slice(i * hd, (i + 1) * hd)
        sc = _dot_nt(q[:, cols].astype(BF16), k_ref[0, :, cols]) * (hd ** -0.5)
        p = jnp.exp(sc - jnp.max(sc, axis=-1, keepdims=True))
        o = _dot(p.astype(BF16), v_ref[0, :, cols])
        heads.append((o / jnp.sum(p, axis=-1, keepdims=True)).astype(BF16))
    x = x + _dot(jnp.concatenate(heads, axis=-1), wo_ref[...])
    h2 = _rms(x, gff_ref[...]).astype(BF16)
    acc = jnp.zeros_like(x)
    for c0 in range(0, ff, ff_chunk):
        g = _dot(h2, w13_ref[:, c0:c0 + ff_chunk])
        u = _dot(h2, w13_ref[:, ff + c0:ff + c0 + ff_chunk])
        acc = acc + _dot((_silu(g) * u).astype(BF16), w2_ref[c0:c0 + ff_chunk, :])
    x = x + acc
    if final_norm:
        x = _rms(x, gfin_ref[...])
    o_ref[0] = x


def _xattn_ffn(x, k_mem, v_mem, g_xa, wq, wo, g_ff, w13, w2, g_fin, *, final_norm, tm, mix=(), w_mix=None):
    bn, s, d = x.shape
    mix_specs = [pl.BlockSpec((1, tm, a.shape[-1]), lambda b, i: (b, i, 0)) for a in mix]
    mix_args = list(mix)
    if mix:
        mix_specs.append(_const_spec(w_mix.shape))
        mix_args.append(w_mix.astype(BF16))
    m_len = k_mem.shape[1]
    ff = w2.shape[0]
    ff_chunk = ff // 2 if (ff // 2) % LANES == 0 else ff
    row = lambda a: a.reshape(1, d)
    kern = functools.partial(_xattn_ffn_kernel, final_norm=final_norm, ff_chunk=ff_chunk, mix_parts=len(mix))
    return pl.pallas_call(
        kern,
        grid=(bn, s // tm),
        in_specs=mix_specs + [
                  pl.BlockSpec((1, tm, d), lambda b, i: (b, i, 0)),
                  pl.BlockSpec((1, m_len, d), lambda b, i: (b, 0, 0)),
                  pl.BlockSpec((1, m_len, d), lambda b, i: (b, 0, 0)),
                  _const_spec((1, d)), _const_spec((d, d)), _const_spec((d, d)),
                  _const_spec((1, d)), _const_spec((d, 2 * ff)), _const_spec((ff, d)),
                  _const_spec((1, d))],
        out_specs=pl.BlockSpec((1, tm, d), lambda b, i: (b, i, 0)),
        out_shape=jax.ShapeDtypeStruct(x.shape, F32),
        compiler_params=_params(),
        name="xattn_ffn_final" if final_norm else "xattn_ffn",
    )(*mix_args, x, k_mem, v_mem, row(g_xa), wq.astype(BF16), wo.astype(BF16), row(g_ff),
      w13.astype(BF16), w2.astype(BF16), row(g_fin))


def _hgrn_tables():
    n = A_CHUNK
    t = np.arange(n)[:, None]
    u = np.arange(n)[None, :]
    sums, pair = [], []
    for lvl in range(HGRN_LEVELS):
        c = n >> (lvl + 1)
        mid = (t // (2 * c)) * (2 * c) + c
        late = t >= mid
        sums.append(np.where(late, (u >= mid) & (u <= t), (u > t) & (u < mid)))
        pair.append(late & (u < mid) & (u >= mid - c))
    sums.append(u <= t)
    sums.append(u > t)
    pair.append(t == u)
    return (np.concatenate(sums, axis=0).astype(np.float32),
            np.stack(pair, axis=0).astype(np.float32))


def _even_kernel(x_ref, lbp_ref, g_ref, win_ref, aon_ref, conv_ref, wout_ref, sums_ref, pair_ref,
                 o_ref, q_scr, k_scr, v_scr, lfh_scr, lfl_scr, oa_scr, z_scr, st_scr):
    tm = x_ref.shape[1]
    n = A_CHUNK

    @pl.when(pl.program_id(1) == 0)
    def _():
        st_scr[...] = jnp.zeros_like(st_scr)
        z_scr[0:8, :] = jnp.zeros((8, B_WIDTH), F32)

    x = x_ref[0]
    h = _rms(x, g_ref[...]).astype(BF16)

    def proj(j, width=A_WIDTH):
        return _dot(h, win_ref[:, j * width:(j + 1) * width])

    lbp = lbp_ref[...]
    e = jnp.exp(lbp - jnp.max(lbp, axis=0, keepdims=True))
    lb = e[0:1, :] / jnp.sum(e, axis=0, keepdims=True)

    q_scr[...] = _silu(proj(0))
    f = lb + (1.0 - lb) * jax.nn.sigmoid(proj(1))
    k_scr[...] = 1.0 - f
    logf = jnp.log(f)
    lf_hi = logf.astype(BF16)
    lfh_scr[...] = lf_hi
    lfl_scr[...] = (logf - lf_hi.astype(F32)).astype(BF16)
    v_scr[...] = proj(2)

    row = lax.broadcasted_iota(I32, (n, 1), 0)
    causal = jnp.where(row >= lax.broadcasted_iota(I32, (1, n), 1), 1.0, 0.0)

    def chunk(c, carry, direct):
        r0 = pl.multiple_of(c * n, n)
        rows = pl.ds(r0, n)
        lf_hi, lf_lo = lfh_scr[rows, :], lfl_scr[rows, :]
        qc, kc, vc = q_scr[rows, :], k_scr[rows, :], v_scr[rows, :]
        edge = sums_ref[HGRN_LEVELS * n:, :]
        d_edge = _dot(edge, lf_hi) + _dot(edge, lf_lo)
        b = d_edge[:n]
        e_in, e_out = jnp.exp(b), jnp.exp(d_edge[n:])
        if direct:
            e_neg = jnp.exp(-b)
        else:
            tree = sums_ref[:HGRN_LEVELS * n, :]
            ex = jnp.exp(_dot(tree, lf_hi) + _dot(tree, lf_lo))
        for hh in range(A_HEADS):
            cols = slice(hh * A_KDIM, (hh + 1) * A_KDIM)
            qh, kh = qc[:, cols], kc[:, cols]
            qin = (qh * e_in[:, cols]).astype(BF16)
            if direct:
                attn = _dot_nt(qin, (kh * e_neg[:, cols]).astype(BF16)) * causal
            else:
                attn = _dot_nt(qh.astype(BF16), kh.astype(BF16)) * pair_ref[HGRN_LEVELS]
                for lvl in range(HGRN_LEVELS):
                    late = (row & (n >> (lvl + 1))) != 0
                    xl = (jnp.where(late, qh, kh) * ex[lvl * n:(lvl + 1) * n, cols]).astype(BF16)
                    attn = attn + _dot_nt(xl, xl) * pair_ref[lvl]
            st = st_scr[hh]
            vh = vc[:, cols].astype(BF16)
            oa_scr[rows, cols] = _dot(attn.astype(BF16), vh) + _dot_nt(qin, st.astype(BF16))
            st_scr[hh] = st * e_in[n - 1:n, cols] + _dot_tn(vh, (kh * e_out[:, cols]).astype(BF16))
        return carry

    decay = -jnp.sum(logf.reshape(tm // n, n, A_WIDTH), axis=1)
    mild = jnp.max(decay) <= HGRN_DIRECT_MAX_DECAY

    @pl.when(mild)
    def _():
        lax.fori_loop(0, tm // n, functools.partial(chunk, direct=True), 0, unroll=2)

    @pl.when(jnp.logical_not(mild))
    def _():
        lax.fori_loop(0, tm // n, functools.partial(chunk, direct=False), 0, unroll=2)

    gate = _silu(proj(3))
    parts = []
    for hh in range(A_HEADS):
        cols = slice(hh * A_VDIM, (hh + 1) * A_VDIM)
        parts.append((_rms(oa_scr[:, cols], aon_ref[hh:hh + 1, :]) * gate[:, cols]).astype(BF16))

    z_scr[8:8 + tm, :] = proj(5) * proj(6)
    w = conv_ref[...]
    y = (w[0:1, :] * z_scr[6:6 + tm, :] + w[1:2, :] * z_scr[7:7 + tm, :] + w[2:3, :] * z_scr[8:8 + tm, :])
    z_scr[0:8, :] = z_scr[tm:tm + 8, :]
    parts.append((proj(4) * y).astype(BF16))

    o_ref[0] = x + _dot(jnp.concatenate(parts, axis=-1), wout_ref[...])


def _even_mixer(x, lbp, g, w_in, a_out_norm, b_conv, w_out, *, tm):
    bn, s, d = x.shape
    sums, pair = _hgrn_tables()
    n_in = w_in.shape[1]
    return pl.pallas_call(
        _even_kernel,
        grid=(bn, s // tm),
        in_specs=[pl.BlockSpec((1, tm, d), lambda b, i: (b, i, 0)),
                  _const_spec(lbp.shape), _const_spec((1, d)), _const_spec((d, n_in)),
                  _const_spec(a_out_norm.shape), _const_spec(b_conv.shape),
                  _const_spec(w_out.shape), _const_spec(sums.shape), _const_spec(pair.shape)],
        out_specs=pl.BlockSpec((1, tm, d), lambda b, i: (b, i, 0)),
        out_shape=jax.ShapeDtypeStruct(x.shape, F32),
        scratch_shapes=[pltpu.VMEM((tm, A_WIDTH), F32), pltpu.VMEM((tm, A_WIDTH), F32),
                        pltpu.VMEM((tm, A_WIDTH), F32), pltpu.VMEM((tm, A_WIDTH), BF16),
                        pltpu.VMEM((tm, A_WIDTH), BF16), pltpu.VMEM((tm, A_WIDTH), F32),
                        pltpu.VMEM((tm + 8, B_WIDTH), F32),
                        pltpu.VMEM((A_HEADS, A_VDIM, A_KDIM), F32)],
        compiler_params=_params(),
        name="even_mixer",
    )(x, lbp, g.reshape(1, d), w_in.astype(BF16), a_out_norm, b_conv, w_out.astype(BF16),
      jnp.asarray(sums, BF16), jnp.asarray(pair, F32))


def _odd_proj_kernel(x_ref, g_ref, win_ref, wwt_ref, qg_ref, kvg_ref, wuq_ref, wukt_ref, iwqt_ref, ikg_ref, ikb_ref,
                     vg_ref, vb_ref, ws_ref, bs_ref,
                     qlat_t_ref, qidx_t_ref, widx_t_ref, ckv_ref, ckv_t_ref, kidx_ref, od_ref):
    tm = x_ref.shape[1]
    tq = DSA_QUERIES
    kc = ckv_t_ref.shape[-1]
    h = _rms(x_ref[0], g_ref[...]).astype(BF16)
    o = 0

    def proj(width):
        nonlocal o
        r = _dot(h, win_ref[:, o:o + width])
        o += width
        return r

    cq = _rms(proj(C_Q_RANK), qg_ref[...]).astype(BF16)
    ckv = _rms(proj(C_KV_RANK), kvg_ref[...])
    ckv_ref[0] = ckv.astype(BF16)
    for j in range(tm // kc):
        ckv_t_ref[0, j, :C_KV_RANK, :] = ckv[j * kc:(j + 1) * kc, :].T.astype(BF16)
        ckv_t_ref[0, j, C_KV_RANK:, :] = jnp.ones((DSA_SUM_ROWS, kc), BF16)
    du = proj(D_WIDTH)
    dv = proj(D_WIDTH)
    kidx_ref[0] = _layer_norm(proj(LANES)[:, :C_IDX_DIM], ikg_ref[...], ikb_ref[...]).astype(BF16)
    widx_t_ref[0] = _dot_nt(wwt_ref[...], h)[:C_IDX_HEADS, :] * (C_IDX_HEADS ** -0.5)

    q = _dot(cq, wuq_ref[...])
    for i in range(C_HEADS):
        qh = q[:, i * C_HEAD_DIM:(i + 1) * C_HEAD_DIM].astype(BF16)
        lat_t = (_dot_nt(wukt_ref[i], qh) * (C_HEAD_DIM ** -0.5 * LOG2E)).astype(BF16)
        idx_t = (_dot_nt(iwqt_ref[i], cq) * (C_IDX_DIM ** -0.5)).astype(BF16)
        g, k = divmod(i, DSA_GROUP_HEADS)
        for j in range(tm // tq):
            qlat_t_ref[0, j, g, :, k * tq:(k + 1) * tq] = lat_t[:, j * tq:(j + 1) * tq]
            qidx_t_ref[0, j, :, i * tq:(i + 1) * tq] = idx_t[:, j * tq:(j + 1) * tq]

    u = _gelu(du)
    v = _layer_norm(_gelu(dv), vg_ref[...], vb_ref[...]).astype(BF16)
    tri = (lax.broadcasted_iota(I32, (D_CHUNK, D_CHUNK), 0) >= lax.broadcasted_iota(I32, (D_CHUNK, D_CHUNK), 1))
    gw = D_WIDTH // D_GROUPS
    for gi in range(D_GROUPS):
        wc = jnp.where(tri, ws_ref[gi], 0.0).astype(BF16)
        cols = slice(gi * gw, (gi + 1) * gw)
        for c0 in range(0, tm, D_CHUNK):
            mixed = _dot(wc, v[c0:c0 + D_CHUNK, cols]) + bs_ref[:, gi:gi + 1]
            od_ref[0, c0:c0 + D_CHUNK, cols] = (u[c0:c0 + D_CHUNK, cols] * mixed).astype(BF16)


def _odd_proj(x, g, w_in, q_g, kv_g, w_uq, w_uk, idx_wq, ik_g, ik_b, v_g, v_b, w_s, b_s, *, tm):
    bn, s, d = x.shape
    c0, c1, c2, c3 = C_Q_RANK, C_Q_RANK + C_KV_RANK, C_Q_RANK + C_KV_RANK + C_IDX_DIM, \
        C_Q_RANK + C_KV_RANK + C_IDX_DIM + C_IDX_HEADS
    pad = LANES - C_IDX_DIM
    w = jnp.concatenate([w_in[:, :c1], w_in[:, c3:], w_in[:, c1:c2], jnp.zeros((d, pad), w_in.dtype)],
                        axis=1).astype(BF16)
    ww_t = jnp.concatenate([w_in[:, c2:c3].T, jnp.zeros((16 - C_IDX_HEADS, d), w_in.dtype)], axis=0).astype(BF16)
    wuk_t = jnp.transpose(w_uk, (0, 2, 1)).astype(BF16)
    iwq_t = jnp.transpose(idx_wq.reshape(C_Q_RANK, C_IDX_HEADS, C_IDX_DIM), (1, 2, 0)).astype(BF16)
    kc = min(DSA_KEY_CHUNK, s)
    assert tm % kc == 0 and tm % DSA_QUERIES == 0
    row = lambda a: a.reshape(1, -1)
    shp = lambda n, dt: jax.ShapeDtypeStruct((bn, s, n), dt)
    blk = lambda n: pl.BlockSpec((1, tm, n), lambda b, i: (b, i, 0))
    nq = s // DSA_QUERIES
    heads_q = C_HEADS * DSA_QUERIES
    n_groups, group_q = C_HEADS // DSA_GROUP_HEADS, DSA_GROUP_HEADS * DSA_QUERIES
    qblk = lambda n: pl.BlockSpec((1, tm // DSA_QUERIES, n, heads_q), lambda b, i: (b, i, 0, 0))
    return pl.pallas_call(
        _odd_proj_kernel,
        grid=(bn, s // tm),
        in_specs=[blk(d), _const_spec((1, d)), _const_spec(w.shape), _const_spec(ww_t.shape),
                  _const_spec((1, C_Q_RANK)), _const_spec((1, C_KV_RANK)),
                  _const_spec(w_uq.shape), _const_spec(wuk_t.shape), _const_spec(iwq_t.shape),
                  _const_spec((1, C_IDX_DIM)), _const_spec((1, C_IDX_DIM)),
                  _const_spec((1, D_WIDTH)), _const_spec((1, D_WIDTH)),
                  _const_spec(w_s.shape), _const_spec((D_CHUNK, D_GROUPS))],
        out_specs=[pl.BlockSpec((1, tm // DSA_QUERIES, n_groups, C_KV_RANK, group_q), lambda b, i: (b, i, 0, 0, 0)),
                   qblk(C_IDX_DIM),
                   pl.BlockSpec((1, C_IDX_HEADS, tm), lambda b, i: (b, 0, i)),
                   blk(C_KV_RANK),
                   pl.BlockSpec((1, tm // kc, C_KV_RANK + DSA_SUM_ROWS, kc), lambda b, i: (b, i, 0, 0)),
                   blk(C_IDX_DIM), blk(D_WIDTH)],
        out_shape=[jax.ShapeDtypeStruct((bn, nq, n_groups, C_KV_RANK, group_q), BF16),
                   jax.ShapeDtypeStruct((bn, nq, C_IDX_DIM, heads_q), BF16),
                   jax.ShapeDtypeStruct((bn, C_IDX_HEADS, s), F32),
                   shp(C_KV_RANK, BF16),
                   jax.ShapeDtypeStruct((bn, s // kc, C_KV_RANK + DSA_SUM_ROWS, kc), BF16),
                   shp(C_IDX_DIM, BF16), shp(D_WIDTH, BF16)],
        compiler_params=_params(),
        name="odd_proj",
    )(x, row(g), w, ww_t, row(q_g), row(kv_g), w_uq.astype(BF16), wuk_t, iwq_t,
      row(ik_g), row(ik_b), row(v_g), row(v_b), w_s, b_s.T)


def _bit_transpose32(words):
    a = list(words)
    j, m = 16, 0x0000FFFF
    while j:
        mask = np.int32(np.uint32(m))
        k = 0
        while k < 32:
            t = (a[k] ^ lax.shift_right_logical(a[k + j], j)) & mask
            a[k] = a[k] ^ t
            a[k + j] = a[k + j] ^ (t << j)
            k = (k + j + 1) & ~j
        j >>= 1
        m = (m ^ (m << j)) & 0xFFFFFFFF
    return a


def _order_key(score):
    bits = lax.bitcast_convert_type(score + 0.0, I32)
    return bits ^ ((bits >> 31) & 0x7FFFFFFF)


def _dsa_kernel(qlat_t_ref, qidx_t_ref, widx_t_ref, kidx_ref, ckv_ref, ckv_t_ref, wuv_ref,
                o_ref, key_scr, plane_scr, cand_scr, s_scr, mx_scr, m_scr, acc_scr, *, topk):
    tq = DSA_QUERIES
    kc = key_scr.shape[1]
    i = pl.program_id(1)
    n_chunks = ((i + 1) * tq + kc - 1) // kc
    t_pos = i * tq + lax.broadcasted_iota(I32, (1, tq), 1)

    qidx_t = qidx_t_ref[0, 0]
    widx_t = widx_t_ref[0]

    def score_chunk(c, carry, masked):
        kb = kidx_ref[0, pl.ds(pl.multiple_of(c * kc, kc), kc), :]
        logits = _dot(kb, qidx_t)
        score = jnp.zeros((kc, tq), F32)
        for hh in range(C_IDX_HEADS):
            score = score + jnp.maximum(logits[:, hh * tq:(hh + 1) * tq], 0.0) * widx_t[hh:hh + 1, :]
        key = _order_key(score)
        if masked:
            key_pos = c * kc + lax.broadcasted_iota(I32, (kc, 1), 0)
            key = jnp.where(key_pos <= t_pos, key, INT_MIN)
        key_scr[c] = key
        u = key ^ INT_MIN
        for half in range(words_per_chunk):
            slab = [u[half * SLAB_ROWS + 8 * j:half * SLAB_ROWS + 8 * j + 8, :] for j in range(32)]
            planes = _bit_transpose32(slab)
            for b in range(32):
                plane_scr[b, c * words_per_chunk + half] = planes[31 - b]
        return carry

    words_per_chunk = kc // SLAB_ROWS
    n_words = plane_scr.shape[1]

    @pl.when(i == 0)
    def _():
        plane_scr[...] = jnp.zeros(plane_scr.shape, I32)

    n_unmasked = (i * tq) // kc
    lax.fori_loop(0, n_unmasked, functools.partial(score_chunk, masked=False), 0)
    lax.fori_loop(n_unmasked, n_chunks, functools.partial(score_chunk, masked=True), 0)

    for w in range(n_words):
        cand_scr[w] = jnp.full((8, tq), jnp.where(w < n_chunks * words_per_chunk, -1, 0), I32)

    def bit_pass(it, state):
        prefix, need = state
        b = 31 - it
        ones = jnp.zeros((8, tq), I32)
        for w in range(n_words):
            ones = ones + lax.population_count(cand_scr[w] & plane_scr[b, w])
        ones = jnp.sum(ones, axis=0, keepdims=True).astype(F32)
        keep = ones >= need
        for w in range(n_words):
            cand = cand_scr[w]
            hit = cand & plane_scr[b, w]
            cand_scr[w] = jnp.where(keep, hit, cand ^ hit)
        return jnp.where(keep, prefix | (jnp.int32(1) << b), prefix), jnp.where(keep, need, need - ones)

    prefix, need = lax.fori_loop(0, 32, bit_pass, (jnp.zeros((1, tq), I32), jnp.full((1, tq), topk, F32)))
    thr = jnp.maximum(prefix ^ INT_MIN, INT_MIN + 1)
    n_equal = jnp.zeros((8, tq), I32)
    for w in range(n_words):
        n_equal = n_equal + lax.population_count(cand_scr[w])
    n_equal = jnp.sum(n_equal, axis=0, keepdims=True).astype(F32)

    has_surplus = jnp.logical_and(prefix != 0, n_equal > need)

    @pl.when(jnp.max(jnp.where(has_surplus, 1.0, 0.0)) > 0.0)
    def _():
        keep_equal = jnp.where(has_surplus, need, float(kc) * 1e6)
        lower = (lax.broadcasted_iota(I32, (kc, kc), 0) >= lax.broadcasted_iota(I32, (kc, kc), 1))
        lower = jnp.where(lower, 1.0, 0.0).astype(BF16)

        def body(c, seen):
            keys = key_scr[c]
            eq = keys == thr
            rank = seen + _dot(lower, jnp.where(eq, 1.0, 0.0).astype(BF16))
            key_scr[c] = jnp.where(jnp.logical_and(eq, rank > keep_equal), INT_MIN, keys)
            return rank[kc - 1:kc, :]

        lax.fori_loop(0, n_chunks, body, jnp.zeros((1, tq), F32))

    n_groups = s_scr.shape[1]
    m_scr[...] = jnp.full(m_scr.shape, -jnp.inf, F32)
    acc_scr[...] = jnp.zeros(acc_scr.shape, F32)

    def pipeline_step(c_next, c_now):
        if c_next is not None:
            c_next, slot_next = c_next
            kv = ckv_ref[0, pl.ds(pl.multiple_of(c_next * kc, kc), kc), :]
            bias = jnp.where(key_scr[c_next] >= thr, 0.0, -jnp.inf)
            bias = jnp.concatenate([bias] * DSA_GROUP_HEADS, axis=1)
        if c_now is not None:
            c_now, slot_now = c_now
            kv_t = ckv_t_ref[0, c_now]
        for g in range(n_groups):
            if c_next is not None:
                sb = _dot(kv, qlat_t_ref[0, 0, g]) + bias
                s_scr[slot_next, g] = sb
                mx_scr[slot_next, g] = jnp.max(sb, axis=0, keepdims=True)
            if c_now is not None:
                m_old = m_scr[g]
                m_new = jnp.maximum(m_old, mx_scr[slot_now, g])
                m_safe = jnp.where(m_new == -jnp.inf, 0.0, m_new)
                alpha = jnp.exp2(m_old - m_safe)
                p = jnp.exp2(s_scr[slot_now, g] - m_safe)
                m_scr[g] = m_new
                acc_scr[g] = alpha * acc_scr[g] + _dot(kv_t, p.astype(BF16))

    pipeline_step((jnp.int32(0), 0), None)
    n_pairs = (n_chunks - 1) // 2

    def attn_pair(j, carry):
        c = 2 * j
        pipeline_step((c + 1, 1), (c, 0))
        pipeline_step((c + 2, 0), (c + 1, 1))
        return carry

    lax.fori_loop(0, n_pairs, attn_pair, 0)
    last = n_chunks - 1

    @pl.when(last == 2 * n_pairs)
    def _():
        pipeline_step(None, (last, 0))

    @pl.when(last != 2 * n_pairs)
    def _():
        pipeline_step((last, 1), (last - 1, 0))
        pipeline_step(None, (last, 1))
    outs = []
    for hh in range(C_HEADS):
        g, k = divmod(hh, DSA_GROUP_HEADS)
        cols = slice(k * tq, (k + 1) * tq)
        o_lat_t = acc_scr[g, :C_KV_RANK, cols] / acc_scr[g, C_KV_RANK:C_KV_RANK + 1, cols]
        outs.append(_dot_tn(o_lat_t.astype(BF16), wuv_ref[hh]).astype(BF16))
    o_ref[0] = jnp.concatenate(outs, axis=-1)


def _dsa(q_lat_t, q_idx_t, w_idx_t, k_idx, ckv, ckv_t, w_uv):
    bn, s, _ = ckv.shape
    tq = DSA_QUERIES
    kc = ckv_t.shape[-1]
    topk = min(C_MAX_TOPK, s // 4)
    heads_q = C_HEADS * tq
    n_groups, group_q = q_lat_t.shape[2], q_lat_t.shape[4]
    full = lambda n: pl.BlockSpec((1, s, n), lambda b, i: (b, 0, 0))
    width = C_HEADS * C_HEAD_DIM
    return pl.pallas_call(
        functools.partial(_dsa_kernel, topk=float(topk)),
        grid=(bn, s // tq),
        in_specs=[pl.BlockSpec((1, 1, n_groups, C_KV_RANK, group_q), lambda b, i: (b, i, 0, 0, 0)),
                  pl.BlockSpec((1, 1, C_IDX_DIM, heads_q), lambda b, i: (b, i, 0, 0)),
                  pl.BlockSpec((1, C_IDX_HEADS, tq), lambda b, i: (b, 0, i)),
                  full(C_IDX_DIM), full(C_KV_RANK),
                  pl.BlockSpec((1, s // kc, ckv_t.shape[2], kc), lambda b, i: (b, 0, 0, 0)),
                  _const_spec(w_uv.shape)],
        out_specs=pl.BlockSpec((1, tq, width), lambda b, i: (b, i, 0)),
        out_shape=jax.ShapeDtypeStruct((bn, s, width), BF16),
        scratch_shapes=[pltpu.VMEM((s // kc, kc, tq), I32), pltpu.VMEM((32, s // SLAB_ROWS, 8, tq), I32),
                        pltpu.VMEM((s // SLAB_ROWS, 8, tq), I32), pltpu.VMEM((2, n_groups, kc, group_q), F32),
                        pltpu.VMEM((2, n_groups, 1, group_q), F32),
                        pltpu.VMEM((n_groups, 1, group_q), F32),
                        pltpu.VMEM((n_groups, ckv_t.shape[2], group_q), F32)],
        compiler_params=_params(),
        name="dsa",
    )(q_lat_t, q_idx_t, w_idx_t, k_idx, ckv, ckv_t, w_uv.astype(BF16))


def _row_tile(s, want):
    return want if s % want == 0 else s


def kernel(x, mem, hgrn_lower_bounds, even_mix_norm, even_w_in, even_a_out_norm, even_b_conv, even_w_out, odd_mix_norm, odd_w_in, odd_c_q_norm, odd_c_kv_norm, odd_c_w_uq, odd_c_w_uk, odd_c_w_uv, odd_c_idx_wq, odd_c_idx_k_g, odd_c_idx_k_b, odd_d_v_g, odd_d_v_b, odd_d_w_s, odd_d_b_s, odd_w_out, xa_norm, xa_mem_norm, xa_wq, xa_wkv, xa_wo, ffn_norm, ffn_w13, ffn_w2, final_norm):
    s = x.shape[1]
    tm = _row_tile(s, 512)
    k_mem, v_mem = _mem_kv(mem, xa_mem_norm, xa_wkv)

    def xattn_ffn(x, layer, last, **mix):
        return _xattn_ffn(x, k_mem[layer], v_mem[layer], xa_norm[layer], xa_wq[layer], xa_wo[layer],
                          ffn_norm[layer], ffn_w13[layer], ffn_w2[layer], final_norm,
                          final_norm=last, tm=tm, **mix)

    x = _even_mixer(x, hgrn_lower_bounds, even_mix_norm[0], even_w_in[0], even_a_out_norm[0],
                    even_b_conv[0], even_w_out[0], tm=tm)
    x = xattn_ffn(x, 0, False)
    q_lat_t, q_idx_t, w_idx_t, ckv, ckv_t, k_idx, o_d = _odd_proj(
        x, odd_mix_norm[0], odd_w_in[0], odd_c_q_norm[0], odd_c_kv_norm[0], odd_c_w_uq[0], odd_c_w_uk[0],
        odd_c_idx_wq[0], odd_c_idx_k_g[0], odd_c_idx_k_b[0], odd_d_v_g[0], odd_d_v_b[0],
        odd_d_w_s[0], odd_d_b_s[0], tm=tm)
    o_c = _dsa(q_lat_t, q_idx_t, w_idx_t, k_idx, ckv, ckv_t, odd_c_w_uv[0])
    return xattn_ffn(x, 1, True, mix=(o_c, o_d), w_mix=odd_w_out[0])
```
